```python
import math
import jax
import jax.numpy as jnp
from jax import lax
import numpy as np

D_MODEL = 1024
BATCH = 8
SEQ = 2048
DEPTH = 2

D_MIX = D_MODEL
GROUP_W = D_MIX // 4
HEAD_DIM = 64
N_HEADS = GROUP_W // HEAD_DIM
CHUNK = 64
CONV_K = 4
SSM_STATE = 128
SSM_GROUPS = 2
SSM_GROUP_W = GROUP_W // SSM_GROUPS
RWKV_DECAY_RANK = 64
RWKV_ICLR_RANK = 64
NORM_EPS = 1e-6
L2_EPS = 1e-6
RWKV_GN_EPS = 64e-5

HG_COLS = 3 * GROUP_W
GDN_QKV = 3 * GROUP_W
GDN_COLS = GDN_QKV + 2 * N_HEADS
SSM_XBC = GROUP_W + 2 * SSM_GROUPS * SSM_STATE
SSM_COLS = SSM_XBC + N_HEADS
RWKV_COLS = 3 * GROUP_W + RWKV_DECAY_RANK + RWKV_ICLR_RANK
GATE_COLS = D_MIX
IN_COLS = HG_COLS + GDN_COLS + SSM_COLS + RWKV_COLS + GATE_COLS

kernel_name = 'hybrid_hgrn2_gdn_ssd_rwkv7_block'


def _split(t, sizes):
    idx = [int(i) for i in np.cumsum(sizes)[:-1]]
    return jnp.split(t, idx, axis=-1)


def _heads(t, d):
    b, l, _ = t.shape
    return t.reshape(b, l, -1, d).transpose(0, 2, 1, 3)


def _merge(t):
    b, h, l, d = t.shape
    return t.transpose(0, 2, 1, 3).reshape(b, l, h * d)


def _chunks(t):
    b, h, l, d = t.shape
    return t.reshape(b, h, l // CHUNK, CHUNK, d).transpose(2, 0, 1, 3, 4)


def _unchunk(t):
    n, b, h, c, d = t.shape
    return t.transpose(1, 2, 0, 3, 4).reshape(b, h, n * c, d)


def rmsnorm(x, g, eps=NORM_EPS):
    xf = x.astype(jnp.float32)
    y = xf * lax.rsqrt(jnp.mean(xf * xf, axis=-1, keepdims=True) + eps)
    return (y * g.astype(jnp.float32)).astype(x.dtype)


def l2norm(x):
    return x * lax.rsqrt(jnp.sum(x * x, axis=-1, keepdims=True) + L2_EPS)


def causal_dwconv(x, w):
    k = w.shape[0]
    return lax.conv_general_dilated(
        x, w[:, None, :].astype(x.dtype), window_strides=(1,), padding=[(k - 1, 0)],
        dimension_numbers=('NWC', 'WIO', 'NWC'), feature_group_count=x.shape[-1])


def gla_chunked(q, k, v, logf):
    b, h, l, dk = q.shape
    dv = v.shape[-1]
    causal = jnp.tril(jnp.ones((CHUNK, CHUNK), bool))[:, :, None]

    def step(S, inp):
        qi, ki, vi, gi = inp
        cum = jnp.cumsum(gi, axis=-2)
        diff = cum[..., :, None, :] - cum[..., None, :, :]
        dec = jnp.exp(jnp.where(causal, diff, -jnp.inf))
        attn = jnp.einsum('bhtd,bhsd,bhtsd->bhts', qi, ki, dec)
        last = cum[..., -1:, :]
        o = (jnp.einsum('bhts,bhsv->bhtv', attn, vi)
             + jnp.einsum('bhtd,bhdv->bhtv', qi * jnp.exp(cum), S))
        S = (S * jnp.exp(last)[..., 0, :, None]
             + jnp.einsum('bhsd,bhsv->bhdv', ki * jnp.exp(last - cum), vi))
        return S, o

    S0 = jnp.zeros((b, h, dk, dv), jnp.float32)
    _, o = lax.scan(step, S0, (_chunks(q), _chunks(k), _chunks(v), _chunks(logf)))
    return _unchunk(o)


def gated_delta_chunked(q, k, v, beta, g):
    b, h, _, dk = q.shape
    dv = v.shape[-1]
    qc, kc, vc, bc, gc = map(_chunks, (q, k, v, beta, g))
    cum = jnp.cumsum(gc[..., 0], axis=-1)
    causal = jnp.tril(jnp.ones((CHUNK, CHUNK), bool))
    strict = jnp.tril(jnp.ones((CHUNK, CHUNK), bool), -1)
    gam = jnp.exp(jnp.where(causal, cum[..., :, None] - cum[..., None, :], -jnp.inf))
    kb = kc * bc
    a_strict = jnp.where(strict, jnp.einsum('nbhid,nbhjd->nbhij', kb, kc) * gam, 0.0)
    t_mat = a_strict + jnp.eye(CHUNK, dtype=jnp.float32)
    rhs = jnp.concatenate([vc * bc, kb * jnp.exp(cum)[..., None]], axis=-1)
    sol = lax.linalg.triangular_solve(t_mat, rhs, left_side=True, lower=True, unit_diagonal=True)
    u, w = sol[..., :dv], sol[..., dv:]
    qk = jnp.where(causal, jnp.einsum('nbhid,nbhjd->nbhij', qc, kc) * gam, 0.0)
    q_dec = qc * jnp.exp(cum)[..., None]
    k_dec = kc * jnp.exp(cum[..., -1:] - cum)[..., None]
    last = jnp.exp(cum[..., -1])[..., None, None]

    def step(S, inp):
        u_i, w_i, qk_i, qd_i, kd_i, last_i = inp
        v_new = u_i - jnp.einsum('bhcd,bhdv->bhcv', w_i, S)
        o = (jnp.einsum('bhcd,bhdv->bhcv', qd_i, S)
             + jnp.einsum('bhij,bhjv->bhiv', qk_i, v_new))
        S = S * last_i + jnp.einsum('bhcd,bhcv->bhdv', kd_i, v_new)
        return S, o

    S0 = jnp.zeros((b, h, dk, dv), jnp.float32)
    _, o = lax.scan(step, S0, (u, w, qk, q_dec, k_dec, last))
    return _unchunk(o)


def ssd_chunked(xv, bh, ch, ga):
    b, h, _, p = xv.shape
    n_state = bh.shape[-1]
    xc, bc, cc, gc = map(_chunks, (xv, bh, ch, ga))
    cum = jnp.cumsum(gc[..., 0], axis=-1)
    causal = jnp.tril(jnp.ones((CHUNK, CHUNK), bool))
    lmat = jnp.exp(jnp.where(causal, cum[..., :, None] - cum[..., None, :], -jnp.inf))
    scores = jnp.einsum('cbhis,cbhjs->cbhij', cc, bc) * lmat
    y_diag = jnp.einsum('cbhij,cbhjp->cbhip', scores, xc)
    states = jnp.einsum('cbhjs,cbhjp->cbhsp', bc * jnp.exp(cum[..., -1:] - cum)[..., None], xc)
    chunk_decay = jnp.exp(cum[..., -1])[..., None, None]

    def step(S, inp):
        st, dec = inp
        return S * dec + st, S

    S0 = jnp.zeros((b, h, n_state, p), jnp.float32)
    _, s_in = lax.scan(step, S0, (states, chunk_decay))
    y_off = jnp.einsum('cbhis,cbhsp->cbhip', cc * jnp.exp(cum)[..., None], s_in)
    return _unchunk(y_diag + y_off)


def rwkv7_scan(r, k, v, w, kk, a):
    b, h, _, d = r.shape
    xs = tuple(t.transpose(2, 0, 1, 3) for t in (r, k, v, w, kk, a))

    def step(S, inp):
        rt, kt, vt, wt, kkt, at = inp
        sa = jnp.einsum('bhvk,bhk->bhv', S, -kkt)
        S = (S * wt[:, :, None, :] + sa[..., None] * (kkt * at)[:, :, None, :]
             + vt[..., None] * kt[:, :, None, :])
        return S, jnp.einsum('bhvk,bhk->bhv', S, rt)

    S0 = jnp.zeros((b, h, d, d), jnp.float32)
    _, o = lax.scan(step, S0, xs)
    return o.transpose(1, 2, 0, 3)


def hgrn2_mixer(proj, lb, norm_g):
    q, f, i = _split(proj.astype(jnp.float32), [GROUP_W] * 3)
    q = jax.nn.silu(q)
    f = lb + (1.0 - lb) * jax.nn.sigmoid(f)
    k = 1.0 - f
    o = gla_chunked(_heads(q, HEAD_DIM) * HEAD_DIM ** -0.5, _heads(k, HEAD_DIM),
                    _heads(i, HEAD_DIM), _heads(jnp.log(f), HEAD_DIM))
    return _merge(rmsnorm(o, norm_g))


def gated_deltanet_mixer(proj, conv_w, a_log, dt_bias, norm_g):
    proj = proj.astype(jnp.float32)
    qkv, bt, at = _split(proj, [GDN_QKV, N_HEADS, N_HEADS])
    qkv = jax.nn.silu(causal_dwconv(qkv, conv_w))
    q, k, v = _split(qkv, [GROUP_W] * 3)
    q = l2norm(_heads(q, HEAD_DIM)) * HEAD_DIM ** -0.5
    k = l2norm(_heads(k, HEAD_DIM))
    v = _heads(v, HEAD_DIM)
    beta = jax.nn.sigmoid(bt).transpose(0, 2, 1)[..., None]
    g = (-jnp.exp(a_log.astype(jnp.float32))
         * jax.nn.softplus(at + dt_bias)).transpose(0, 2, 1)[..., None]
    o = gated_delta_chunked(q, k, v, beta, g)
    return _merge(rmsnorm(o, norm_g))


def mamba2_mixer(proj, z, conv_w, conv_b, a_log, dt_bias, d_skip, norm_g):
    proj = proj.astype(jnp.float32)
    xbc, dt = _split(proj, [SSM_XBC, N_HEADS])
    xbc = jax.nn.silu(causal_dwconv(xbc, conv_w) + conv_b)
    xs, bm, cm = _split(xbc, [GROUP_W, SSM_GROUPS * SSM_STATE, SSM_GROUPS * SSM_STATE])
    rep = N_HEADS // SSM_GROUPS
    xh = _heads(xs, HEAD_DIM)
    bh = jnp.repeat(_heads(bm, SSM_STATE), rep, axis=1)
    ch = jnp.repeat(_heads(cm, SSM_STATE), rep, axis=1)
    dt = jax.nn.softplus(dt + dt_bias).transpose(0, 2, 1)[..., None]
    a = -jnp.exp(a_log.astype(jnp.float32))[:, None, None]
    y = ssd_chunked(xh * dt, bh, ch, dt * a)
    y = y + xh * d_skip[:, None, None]
    y = _merge(y) * jax.nn.silu(z)
    b, l, _ = y.shape
    y = rmsnorm(y.reshape(b, l, SSM_GROUPS, SSM_GROUP_W), norm_g.reshape(SSM_GROUPS, SSM_GROUP_W))
    return y.reshape(b, l, GROUP_W)


def rwkv7_mixer(proj, mu, w0, w_up, a0, a_up, k_k, k_a, r_k, ln_w, ln_b):
    proj = proj.astype(jnp.float32)
    prev = jnp.pad(proj[:, :-1], ((0, 0), (1, 0), (0, 0)))
    proj = proj + (prev - proj) * mu
    r, k, v, wd, ad = _split(proj, [GROUP_W, GROUP_W, GROUP_W, RWKV_DECAY_RANK, RWKV_ICLR_RANK])
    w = -jax.nn.softplus(-(w0 + jnp.tanh(wd) @ w_up)) - 0.5
    decay = jnp.exp(-jnp.exp(w))
    a = jax.nn.sigmoid(a0 + ad @ a_up)
    kk = l2norm(_heads(k * k_k, HEAD_DIM))
    k = k * (1.0 + (a - 1.0) * k_a)
    rh, kh, vh = _heads(r, HEAD_DIM), _heads(k, HEAD_DIM), _heads(v, HEAD_DIM)
    o = rwkv7_scan(rh, kh, vh, _heads(decay, HEAD_DIM), kk, _heads(a, HEAD_DIM))
    hs = (N_HEADS, 1, HEAD_DIM)
    mean = jnp.mean(o, axis=-1, keepdims=True)
    var = jnp.mean(jnp.square(o - mean), axis=-1, keepdims=True)
    o = (o - mean) * lax.rsqrt(var + RWKV_GN_EPS) * ln_w.reshape(hs) + ln_b.reshape(hs)
    bonus = jnp.sum(rh * kh * r_k.reshape(hs), axis=-1, keepdims=True) * vh
    return _merge(o + bonus)


def setup_inputs(seed: int = 0) -> dict:
    key = jax.random.key(seed)
    ks = jax.random.split(key, 32)
    f32 = jnp.float32

    def nrm(k, shape, s):
        return s * jax.random.normal(k, shape, f32)

    def gain(k, shape):
        return 1.0 + 0.02 * jax.random.normal(k, shape, f32)

    def dt_bias(k):
        dt = jnp.exp(jax.random.uniform(k, (DEPTH, N_HEADS), f32, math.log(1e-3), math.log(1e-1)))
        return dt + jnp.log(-jnp.expm1(-dt))

    def a_log(k):
        return jnp.log(jax.random.uniform(k, (DEPTH, N_HEADS), f32, 1.0, 16.0))

    return {
        'x': jax.random.normal(ks[0], (BATCH, SEQ, D_MODEL), f32),
        'pre_norm_g': gain(ks[1], (DEPTH, D_MODEL)),
        'w_in': nrm(ks[2], (DEPTH, D_MODEL, IN_COLS), D_MODEL ** -0.5),
        'hgrn_lower_bounds': nrm(ks[3], (DEPTH, GROUP_W), 0.1),
        'hgrn_norm_g': gain(ks[4], (DEPTH, HEAD_DIM)),
        'gdn_conv_w': nrm(ks[5], (DEPTH, CONV_K, GDN_QKV), 0.5),
        'gdn_a_log': a_log(ks[6]),
        'gdn_dt_bias': dt_bias(ks[7]),
        'gdn_norm_g': gain(ks[8], (DEPTH, HEAD_DIM)),
        'ssm_conv_w': nrm(ks[9], (DEPTH, CONV_K, SSM_XBC), 0.5),
        'ssm_conv_b': nrm(ks[10], (DEPTH, SSM_XBC), 0.02),
        'ssm_a_log': a_log(ks[11]),
        'ssm_dt_bias': dt_bias(ks[12]),
        'ssm_d': gain(ks[13], (DEPTH, N_HEADS)),
        'ssm_norm_g': gain(ks[14], (DEPTH, GROUP_W)),
        'rwkv_mu': jax.random.uniform(ks[15], (DEPTH, RWKV_COLS), f32),
        'rwkv_w0': jax.random.uniform(ks[16], (DEPTH, GROUP_W), f32, -6.0, -1.0),
        'rwkv_w_up': nrm(ks[17], (DEPTH, RWKV_DECAY_RANK, GROUP_W), 0.1),
        'rwkv_a0': nrm(ks[18], (DEPTH, GROUP_W), 0.1),
        'rwkv_a_up': nrm(ks[19], (DEPTH, RWKV_ICLR_RANK, GROUP_W), 0.1),
        'rwkv_k_k': 0.85 + nrm(ks[20], (DEPTH, GROUP_W), 0.02),
        'rwkv_k_a': gain(ks[21], (DEPTH, GROUP_W)),
        'rwkv_r_k': nrm(ks[22], (DEPTH, GROUP_W), 0.1),
        'rwkv_ln_w': gain(ks[23], (DEPTH, GROUP_W)),
        'rwkv_ln_b': nrm(ks[24], (DEPTH, GROUP_W), 0.02),
        'w_out': nrm(ks[25], (DEPTH, D_MIX, D_MODEL), D_MIX ** -0.5),
        'post_norm_g': gain(ks[26], (DEPTH, D_MODEL)),
    }


def reference(x, pre_norm_g, w_in, hgrn_lower_bounds, hgrn_norm_g, gdn_conv_w, gdn_a_log,
              gdn_dt_bias, gdn_norm_g, ssm_conv_w, ssm_conv_b, ssm_a_log, ssm_dt_bias, ssm_d,
              ssm_norm_g, rwkv_mu, rwkv_w0, rwkv_w_up, rwkv_a0, rwkv_a_up, rwkv_k_k, rwkv_k_a,
              rwkv_r_k, rwkv_ln_w, rwkv_ln_b, w_out, post_norm_g):
    sm = jax.nn.softmax(hgrn_lower_bounds.astype(jnp.float32), axis=0)
    lb_table = jnp.cumsum(sm, axis=0) - sm[0]
    h = x
    for l in range(DEPTH):
        u = rmsnorm(h, pre_norm_g[l])
        proj = jnp.einsum('bld,dc->blc', u, w_in[l])
        p_a, p_b, p_c, p_d, gates = _split(proj, [HG_COLS, GDN_COLS, SSM_COLS, RWKV_COLS, GATE_COLS])
        z_a, z_b, z_c, z_d = _split(gates.astype(jnp.float32), [GROUP_W] * 4)
        y_a = hgrn2_mixer(p_a, lb_table[l], hgrn_norm_g[l]) * jax.nn.silu(z_a)
        y_b = gated_deltanet_mixer(p_b, gdn_conv_w[l], gdn_a_log[l], gdn_dt_bias[l],
                                   gdn_norm_g[l]) * jax.nn.silu(z_b)
        y_c = mamba2_mixer(p_c, z_c, ssm_conv_w[l], ssm_conv_b[l], ssm_a_log[l], ssm_dt_bias[l],
                           ssm_d[l], ssm_norm_g[l])
        y_d = rwkv7_mixer(p_d, rwkv_mu[l], rwkv_w0[l], rwkv_w_up[l], rwkv_a0[l], rwkv_a_up[l],
                          rwkv_k_k[l], rwkv_k_a[l], rwkv_r_k[l], rwkv_ln_w[l],
                          rwkv_ln_b[l]) * jax.nn.silu(z_d)
        y = jnp.concatenate([y_a, y_b, y_c, y_d], axis=-1).astype(h.dtype)
        out = jnp.einsum('blc,cd->bld', y, w_out[l])
        h = h + rmsnorm(out, post_norm_g[l])
    return h
```

```python
import functools

import jax
import jax.numpy as jnp
from jax import lax
from jax.experimental import pallas as pl
from jax.experimental.pallas import tpu as pltpu

F32 = jnp.float32
BF16 = jnp.bfloat16

D_MODEL = 1024
GROUP_W = 256
HEAD_DIM = 64
N_HEADS = 4
CHUNK = 64
SUB = 16
CONV_K = 4
SSM_STATE = 128
SSM_GROUPS = 2
LOW_RANK = 64
NORM_EPS = 1e-6
L2_EPS = 1e-6
RWKV_GN_EPS = 64e-5

HG_OFF = 0
GDN_OFF = 768
SSM_OFF = 1536
RWKV_OFF = 2304
RWKV_W = 3 * GROUP_W + 2 * LOW_RANK
GATE_OFF = 3200
SMALL_OFF = 4224
PROJ_COLS = 4352
CONV_OFF = GDN_OFF
CONV_COLS = GATE_OFF - GDN_OFF

TAIL = 8
TIME_TILE = 256
V7X_VMEM_LIMIT_BYTES = 52 * 1024 * 1024

(V_HG_G, V_GDN_G, V_SSM_D, V_SSM_G, V_W0, V_A0, V_KK, V_KA, V_RK, V_LNW, V_LNB) = range(11)
V_LB = 11


def _silu(x):
    return x * jax.nn.sigmoid(x)


def _split(a):
    hi = a.astype(BF16)
    lo = (a - hi.astype(F32)).astype(BF16)
    return hi, lo


def _mm(a, b):
    return jnp.dot(a.astype(BF16), b.astype(BF16), preferred_element_type=F32)


def _mm_nt(a, b):
    return lax.dot_general(a.astype(BF16), b.astype(BF16), (((1,), (1,)), ((), ())),
                           preferred_element_type=F32)


def _mm_tn(a, b):
    return lax.dot_general(a.astype(BF16), b.astype(BF16), (((0,), (0,)), ((), ())),
                           preferred_element_type=F32)


def _mm_const_l(c, b):
    hi, lo = _split(b)
    lo2 = (b - hi.astype(F32) - lo.astype(F32)).astype(BF16)
    d = functools.partial(jnp.dot, preferred_element_type=F32)
    return d(c, hi) + d(c, lo) + d(c, lo2)


def _mm_const_r(a, c):
    hi, lo = _split(a)
    lo2 = (a - hi.astype(F32) - lo.astype(F32)).astype(BF16)
    d = functools.partial(jnp.dot, preferred_element_type=F32)
    return d(hi, c) + d(lo, c) + d(lo2, c)


def _mm3(a, b):
    ah, al = _split(a)
    bh, bl = _split(b)
    d = functools.partial(jnp.dot, preferred_element_type=F32)
    return d(ah, bh) + d(ah, bl) + d(al, bh)


def _tri_inverse(n_strict, ri, ci):
    eye = jnp.where(ri == ci, 1.0, 0.0).astype(F32)
    x = None
    s = 1
    while s < CHUNK:
        level = ((ri // (2 * s)) == (ci // (2 * s))) & (((ri // s) % 2) == 1) & (((ci // s) % 2) == 0)
        b = jnp.where(level, n_strict, 0.0)
        x = eye - b if x is None else x - _mm3(x, _mm3(b, x))
        s *= 2
    return x


def _layer_kernel(layer, n_chunks,
                  x_ref, gpre_ref, win_ref, wout_ref, gpost_ref, vec_ref, small_ref,
                  gconv_ref, sconvw_ref, sconvb_ref, mu_ref, wup_ref, aup_ref, ltri_ref, hsb_ref,
                  o_ref,
                  p_ref, y_ref, ohead_ref, s_hg, s_gdn, s_ssm, s_rwkv):
    t_idx = pl.program_id(1)
    tl = n_chunks * CHUNK

    @pl.when(t_idx == 0)
    def _():
        p_ref[0:TAIL, :] = jnp.zeros((TAIL, PROJ_COLS), F32)
        s_hg[...] = jnp.zeros_like(s_hg)
        s_gdn[...] = jnp.zeros_like(s_gdn)
        s_ssm[...] = jnp.zeros_like(s_ssm)
        s_rwkv[...] = jnp.zeros_like(s_rwkv)

    @pl.when(t_idx != 0)
    def _():
        p_ref[0:TAIL, :] = p_ref[tl:tl + TAIL, :]

    h = x_ref[...]
    u = h * lax.rsqrt(jnp.mean(h * h, axis=-1, keepdims=True) + NORM_EPS) * gpre_ref[...]
    p_ref[TAIL:TAIL + tl, :] = jnp.dot(u.astype(BF16), win_ref[...], preferred_element_type=F32)

    ltri = ltri_ref[...]
    hsb = hsb_ref[...]
    ri = lax.broadcasted_iota(jnp.int32, (CHUNK, CHUNK), 0)
    ci = lax.broadcasted_iota(jnp.int32, (CHUNK, CHUNK), 1)
    causal = ri >= ci
    strict = ri > ci
    upper_strict = jnp.where(ri > ci, 1.0, 0.0).astype(F32)
    sub_r = lax.broadcasted_iota(jnp.int32, (SUB, GROUP_W), 0)

    def vec(i):
        return vec_ref[i:i + 1, :]

    depth = vec_ref.shape[0] - V_LB
    lbs = [vec(V_LB + l) for l in range(depth)]
    lb_max = functools.reduce(jnp.maximum, lbs)
    lb_exp = [jnp.exp(v - lb_max) for v in lbs]
    lb_den = functools.reduce(lambda a, b: a + b, lb_exp)
    sm = [e / lb_den for e in lb_exp]
    lb = functools.reduce(lambda a, b: a + b, sm[:layer + 1]) - sm[0]

    small_bias = small_ref[0:1, :]
    small_alog = small_ref[1:2, :]
    neg_a = -jnp.exp(small_alog)

    def head_sum(v):
        return _mm_const_r(v, hsb)

    def decay_diff(gcol):
        return _mm_const_l(ltri, jnp.broadcast_to(gcol, (CHUNK, CHUNK)) * upper_strict)

    def chunk_body(c, carry):
        r0 = pl.multiple_of(c * CHUNK, CHUNK)
        cur = pl.ds(r0 + TAIL, CHUNK)
        win = pl.ds(r0, CHUNK + TAIL)

        gates = p_ref[cur, GATE_OFF:GATE_OFF + 4 * GROUP_W]
        small = p_ref[cur, SMALL_OFF:SMALL_OFF + 128]
        beta_all = jax.nn.sigmoid(small)
        sp_all = jax.nn.softplus(small + small_bias)
        g_all = sp_all * neg_a
        cum_small = _mm_const_l(ltri, g_all)

        hg = p_ref[cur, HG_OFF:HG_OFF + 3 * GROUP_W]
        q = _silu(hg[:, 0:GROUP_W]) * (HEAD_DIM ** -0.5)
        f = lb + (1.0 - lb) * jax.nn.sigmoid(hg[:, GROUP_W:2 * GROUP_W])
        k = 1.0 - f
        v = hg[:, 2 * GROUP_W:3 * GROUP_W]
        cum = _mm_const_l(ltri, jnp.log(f))
        cum_last = cum[CHUNK - 1:CHUNK, :]
        q_in = q * jnp.exp(cum)
        k_out = k * jnp.exp(cum_last - cum)
        for i in range(CHUNK // SUB):
            lo = i * SUB
            cum_i = cum[lo:lo + SUB, :]
            q_i = q[lo:lo + SUB, :]
            zs = []
            for sl in range(SUB):
                s = lo + sl
                e = jnp.exp(jnp.minimum(cum_i - cum[s:s + 1, :], 0.0))
                zs.append(q_i * e * k[s:s + 1, :])
            coef = _mm(jnp.concatenate(zs, axis=0), hsb)
            o_i = jnp.zeros((SUB, GROUP_W), F32)
            for sl in range(SUB):
                s = lo + sl
                cf = jnp.where(sub_r >= sl, coef[sl * SUB:(sl + 1) * SUB, :], 0.0)
                o_i = o_i + cf * v[s:s + 1, :]
            ohead_ref[lo:lo + SUB, :] = o_i
            if i > 0:
                cs = cum[lo - 1:lo, :]
                qd = q_i * jnp.exp(cum_i - cs)
                kh = k[0:lo, :] * jnp.exp(cs - cum[0:lo, :])
                for hd in range(N_HEADS):
                    hs = slice(hd * HEAD_DIM, (hd + 1) * HEAD_DIM)
                    a = _mm_nt(qd[:, hs], kh[:, hs])
                    ohead_ref[lo:lo + SUB, hs] += _mm(a, v[0:lo, hs])
        for hd in range(N_HEADS):
            hs = slice(hd * HEAD_DIM, (hd + 1) * HEAD_DIM)
            st = s_hg[hd]
            ohead_ref[:, hs] += _mm_nt(q_in[:, hs], st)
            s_hg[hd] = st * jnp.exp(cum_last)[:, hs] + _mm_tn(v[:, hs], k_out[:, hs])
        o = ohead_ref[...]
        o = o * lax.rsqrt(head_sum(o * o) * (1.0 / HEAD_DIM) + NORM_EPS) * vec(V_HG_G)
        y_ref[pl.ds(r0, CHUNK), 0:GROUP_W] = (o * _silu(gates[:, 0:GROUP_W])).astype(y_ref.dtype)

        xw = p_ref[win, GDN_OFF:GDN_OFF + 3 * GROUP_W]
        xc = gconv_ref[0:1, :] * xw[TAIL - 3:TAIL - 3 + CHUNK, :]
        for j in range(1, CONV_K):
            xc = xc + gconv_ref[j:j + 1, :] * xw[TAIL - 3 + j:TAIL - 3 + j + CHUNK, :]
        xc = _silu(xc)
        q = xc[:, 0:GROUP_W]
        k = xc[:, GROUP_W:2 * GROUP_W]
        v = xc[:, 2 * GROUP_W:3 * GROUP_W]
        q = q * lax.rsqrt(head_sum(q * q) + L2_EPS) * (HEAD_DIM ** -0.5)
        k = k * lax.rsqrt(head_sum(k * k) + L2_EPS)
        for hd in range(N_HEADS):
            hs = slice(hd * HEAD_DIM, (hd + 1) * HEAD_DIM)
            beta = beta_all[:, hd:hd + 1]
            gcol = g_all[:, 4 + hd:5 + hd]
            ccol = cum_small[:, 4 + hd:5 + hd]
            clast = cum_small[CHUNK - 1:CHUNK, 4 + hd:5 + hd]
            gam = jnp.where(causal, jnp.exp(jnp.minimum(decay_diff(gcol), 0.0)), 0.0)
            qh, kh, vh = q[:, hs], k[:, hs], v[:, hs]
            kb = kh * beta
            n_mat = jnp.where(strict, _mm_nt(kb, kh) * gam, 0.0)
            xinv = _tri_inverse(n_mat, ri, ci)
            ecum = jnp.exp(ccol)
            uu = _mm3(xinv, vh * beta)
            ww = _mm3(xinv, kb * ecum)
            qk = jnp.where(causal, _mm_nt(qh, kh) * gam, 0.0)
            st = s_gdn[hd]
            v_new = uu - _mm(ww, st)
            ohead_ref[:, hs] = _mm(qh * ecum, st) + _mm(qk, v_new)
            s_gdn[hd] = st * jnp.exp(clast) + _mm_tn(kh * jnp.exp(clast - ccol), v_new)
        o = ohead_ref[...]
        o = o * lax.rsqrt(head_sum(o * o) * (1.0 / HEAD_DIM) + NORM_EPS) * vec(V_GDN_G)
        y_ref[pl.ds(r0, CHUNK), GROUP_W:2 * GROUP_W] = (o * _silu(gates[:, GROUP_W:2 * GROUP_W])).astype(y_ref.dtype)

        xw = p_ref[win, SSM_OFF:SSM_OFF + 3 * GROUP_W]
        xc = sconvw_ref[0:1, :] * xw[TAIL - 3:TAIL - 3 + CHUNK, :]
        for j in range(1, CONV_K):
            xc = xc + sconvw_ref[j:j + 1, :] * xw[TAIL - 3 + j:TAIL - 3 + j + CHUNK, :]
        xc = _silu(xc + sconvb_ref[...])
        xs = xc[:, 0:GROUP_W]
        for grp in range(SSM_GROUPS):
            bg = xc[:, GROUP_W + grp * SSM_STATE:GROUP_W + (grp + 1) * SSM_STATE]
            cg = xc[:, 2 * GROUP_W + grp * SSM_STATE:2 * GROUP_W + (grp + 1) * SSM_STATE]
            cb = _mm_nt(cg, bg)
            for hh in range(N_HEADS // SSM_GROUPS):
                hd = grp * (N_HEADS // SSM_GROUPS) + hh
                hs = slice(hd * HEAD_DIM, (hd + 1) * HEAD_DIM)
                dt = sp_all[:, 8 + hd:9 + hd]
                gcol = g_all[:, 8 + hd:9 + hd]
                ccol = cum_small[:, 8 + hd:9 + hd]
                clast = cum_small[CHUNK - 1:CHUNK, 8 + hd:9 + hd]
                lmat = jnp.where(causal, jnp.exp(jnp.minimum(decay_diff(gcol), 0.0)), 0.0)
                xv = xs[:, hs] * dt
                st = s_ssm[hd]
                ohead_ref[:, hs] = _mm(cb * lmat, xv) + jnp.exp(ccol) * _mm(cg, st)
                s_ssm[hd] = st * jnp.exp(clast) + _mm_tn(bg, xv * jnp.exp(clast - ccol))
        yv = (ohead_ref[...] + xs * vec(V_SSM_D)) * _silu(gates[:, 2 * GROUP_W:3 * GROUP_W])
        gw = GROUP_W // SSM_GROUPS
        for grp in range(SSM_GROUPS):
            yg = yv[:, grp * gw:(grp + 1) * gw]
            yg = yg * lax.rsqrt(jnp.mean(yg * yg, axis=-1, keepdims=True) + NORM_EPS)
            yg = yg * vec_ref[V_SSM_G:V_SSM_G + 1, grp * gw:(grp + 1) * gw]
            y_ref[pl.ds(r0, CHUNK), 2 * GROUP_W + grp * gw:2 * GROUP_W + (grp + 1) * gw] = yg.astype(y_ref.dtype)

        xw = p_ref[win, RWKV_OFF:RWKV_OFF + RWKV_W]
        xcur = xw[TAIL:TAIL + CHUNK, :]
        xr = xcur + (xw[TAIL - 1:TAIL - 1 + CHUNK, :] - xcur) * mu_ref[...]
        r = xr[:, 0:GROUP_W]
        k = xr[:, GROUP_W:2 * GROUP_W]
        v = xr[:, 2 * GROUP_W:3 * GROUP_W]
        wd = xr[:, 3 * GROUP_W:3 * GROUP_W + LOW_RANK]
        ad = xr[:, 3 * GROUP_W + LOW_RANK:3 * GROUP_W + 2 * LOW_RANK]
        w = -jax.nn.softplus(-(vec(V_W0) + _mm(jnp.tanh(wd), wup_ref[...]))) - 0.5
        lw = -jnp.exp(w)
        a = jax.nn.sigmoid(vec(V_A0) + _mm(ad, aup_ref[...]))
        kk = k * vec(V_KK)
        kk = kk * lax.rsqrt(head_sum(kk * kk) + L2_EPS)
        k = k * (1.0 + (a - 1.0) * vec(V_KA))
        cum = _mm_const_l(ltri, lw)
        p_inc = jnp.exp(cum)
        p_inv = jnp.exp(-cum)
        p_last = p_inc[CHUNK - 1:CHUNK, :]
        a_t = -kk * jnp.exp(cum - lw)
        b_t = kk * a * p_inv
        k_t = k * p_inv
        r_t = r * p_inc
        b_end = b_t * p_last
        k_end = k_t * p_last
        for hd in range(N_HEADS):
            hs = slice(hd * HEAD_DIM, (hd + 1) * HEAD_DIM)
            ah, bh, kh, rh, vh = a_t[:, hs], b_t[:, hs], k_t[:, hs], r_t[:, hs], v[:, hs]
            ab = jnp.where(strict, _mm_nt(ah, bh), 0.0)
            ak = jnp.where(strict, _mm_nt(ah, kh), 0.0)
            rb = jnp.where(causal, _mm_nt(rh, bh), 0.0)
            rk = jnp.where(causal, _mm_nt(rh, kh), 0.0)
            xinv = _tri_inverse(-ab, ri, ci)
            st = s_rwkv[hd]
            uu = _mm3(xinv, _mm_nt(ah, st) + _mm(ak, vh))
            ohead_ref[:, hs] = _mm_nt(rh, st) + _mm(rb, uu) + _mm(rk, vh)
            s_rwkv[hd] = st * p_last[:, hs] + _mm_tn(uu, b_end[:, hs]) + _mm_tn(vh, k_end[:, hs])
        o = ohead_ref[...]
        mean = head_sum(o) * (1.0 / HEAD_DIM)
        oc = o - mean
        var = head_sum(oc * oc) * (1.0 / HEAD_DIM)
        o = oc * lax.rsqrt(var + RWKV_GN_EPS) * vec(V_LNW) + vec(V_LNB)
        o = o + head_sum(r * k * vec(V_RK)) * v
        y_ref[pl.ds(r0, CHUNK), 3 * GROUP_W:4 * GROUP_W] = (o * _silu(gates[:, 3 * GROUP_W:4 * GROUP_W])).astype(y_ref.dtype)
        return carry

    lax.fori_loop(0, n_chunks, chunk_body, 0)

    out = jnp.dot(y_ref[...], wout_ref[...], preferred_element_type=F32)
    out = out * lax.rsqrt(jnp.mean(out * out, axis=-1, keepdims=True) + NORM_EPS) * gpost_ref[...]
    o_ref[...] = x_ref[...] + out


def _layer_call(layer, h, gpre, win, wout, gpost, vecs, small, gconv, sconvw, sconvb, mu, wup, aup, ltri, hsb):
    b, l, d = h.shape
    tl = min(TIME_TILE, l)
    assert l % tl == 0 and tl % CHUNK == 0 and d == D_MODEL
    n_chunks = tl // CHUNK

    def full(a):
        return pl.BlockSpec(a.shape, lambda i, j: (0,) * a.ndim)

    params = (gpre, win, wout, gpost, vecs, small, gconv, sconvw, sconvb, mu, wup, aup, ltri, hsb)
    return pl.pallas_call(
        functools.partial(_layer_kernel, layer, n_chunks),
        grid=(b, l // tl),
        in_specs=[pl.BlockSpec((None, tl, d), lambda i, j: (i, j, 0))] + [full(a) for a in params],
        out_specs=pl.BlockSpec((None, tl, d), lambda i, j: (i, j, 0)),
        out_shape=jax.ShapeDtypeStruct(h.shape, h.dtype),
        scratch_shapes=[
            pltpu.VMEM((tl + TAIL, PROJ_COLS), F32),
            pltpu.VMEM((tl, D_MODEL), BF16),
            pltpu.VMEM((CHUNK, GROUP_W), F32),
            pltpu.VMEM((N_HEADS, HEAD_DIM, HEAD_DIM), F32),
            pltpu.VMEM((N_HEADS, HEAD_DIM, HEAD_DIM), F32),
            pltpu.VMEM((N_HEADS, SSM_STATE, HEAD_DIM), F32),
            pltpu.VMEM((N_HEADS, HEAD_DIM, HEAD_DIM), F32),
        ],
        compiler_params=pltpu.CompilerParams(
            dimension_semantics=("arbitrary", "arbitrary"),
            vmem_limit_bytes=V7X_VMEM_LIMIT_BYTES),
        name=f"hybrid_layer{layer}",
    )(h, *params)


def _repack_w_in(w):
    hg = w[:, 0:768]
    gdn_qkv = w[:, 768:1536]
    gdn_small = w[:, 1536:1544]
    ssm_xbc = w[:, 1544:2312]
    ssm_dt = w[:, 2312:2316]
    rwkv = w[:, 2316:3212]
    gates = w[:, 3212:4236]
    pad = jnp.zeros((w.shape[0], PROJ_COLS - SMALL_OFF - 12), w.dtype)
    return jnp.concatenate([hg, gdn_qkv, ssm_xbc, rwkv, gates, gdn_small, ssm_dt, pad], axis=1).astype(BF16)


def kernel(x, pre_norm_g, w_in, hgrn_lower_bounds, hgrn_norm_g, gdn_conv_w, gdn_a_log, gdn_dt_bias, gdn_norm_g,
           ssm_conv_w, ssm_conv_b, ssm_a_log, ssm_dt_bias, ssm_d, ssm_norm_g, rwkv_mu, rwkv_w0, rwkv_w_up,
           rwkv_a0, rwkv_a_up, rwkv_k_k, rwkv_k_a, rwkv_r_k, rwkv_ln_w, rwkv_ln_b, w_out, post_norm_g):
    depth = w_in.shape[0]
    ri = lax.broadcasted_iota(jnp.int32, (CHUNK, CHUNK), 0)
    ci = lax.broadcasted_iota(jnp.int32, (CHUNK, CHUNK), 1)
    ltri = (ri >= ci).astype(BF16)
    hr = lax.broadcasted_iota(jnp.int32, (GROUP_W, GROUP_W), 0) // HEAD_DIM
    hc = lax.broadcasted_iota(jnp.int32, (GROUP_W, GROUP_W), 1) // HEAD_DIM
    hsb = (hr == hc).astype(BF16)
    zeros4 = jnp.zeros((N_HEADS,), F32)
    h = x
    for l in range(depth):
        vecs = jnp.stack([
            jnp.tile(hgrn_norm_g[l], N_HEADS), jnp.tile(gdn_norm_g[l], N_HEADS),
            jnp.repeat(ssm_d[l], HEAD_DIM), ssm_norm_g[l], rwkv_w0[l], rwkv_a0[l], rwkv_k_k[l],
            rwkv_k_a[l], rwkv_r_k[l], rwkv_ln_w[l], rwkv_ln_b[l]] + [hgrn_lower_bounds[i] for i in range(depth)])
        pad = jnp.zeros((128 - 12,), F32)
        small = jnp.stack([
            jnp.concatenate([zeros4, gdn_dt_bias[l], ssm_dt_bias[l], pad]),
            jnp.concatenate([zeros4, gdn_a_log[l], ssm_a_log[l], pad])])
        h = _layer_call(
            l, h, pre_norm_g[l][None, :], _repack_w_in(w_in[l]), w_out[l].astype(BF16), post_norm_g[l][None, :],
            vecs.astype(F32), small.astype(F32), gdn_conv_w[l], ssm_conv_w[l], ssm_conv_b[l][None, :],
            rwkv_mu[l][None, :], rwkv_w_up[l].astype(BF16), rwkv_a_up[l].astype(BF16), ltri, hsb)
    return h
```

```python
import functools

import jax
import jax.numpy as jnp
from jax import lax
from jax.experimental import pallas as pl
from jax.experimental.pallas import tpu as pltpu

F32 = jnp.float32
BF16 = jnp.bfloat16

D_MODEL = 1024
GROUP_W = 256
HEAD_DIM = 64
N_HEADS = 4
CHUNK = 64
SUB = 16
CONV_K = 4
SSM_STATE = 128
SSM_GROUPS = 2
LOW_RANK = 64
NORM_EPS = 1e-6
L2_EPS = 1e-6
RWKV_GN_EPS = 64e-5

HG_OFF = 0
GDN_OFF = 768
SSM_OFF = 1536
RWKV_OFF = 2304
RWKV_W = 3 * GROUP_W + 2 * LOW_RANK
GATE_OFF = 3200
SMALL_OFF = 4224
SMALL_W = 128
PROJ_COLS = 4352

TAIL = 8
TIME_TILE = 256
V7X_VMEM_LIMIT_BYTES = 52 * 1024 * 1024

(V_HG_G, V_GDN_G, V_SSM_D, V_SSM_G, V_W0, V_A0, V_KK, V_KA, V_RK, V_LNW, V_LNB) = range(11)
V_LB = 11

M_CAUSAL, M_STRICT, M_EYE, M_LEVEL0 = 0, 1, 2, 3
N_LEVELS = 6


def _silu(x):
    return x * jax.nn.sigmoid(x)


def _split(a):
    hi = a.astype(BF16)
    lo = (a - hi.astype(F32)).astype(BF16)
    return hi, lo


def _split3(a):
    hi, lo = _split(a)
    lo2 = (a - hi.astype(F32) - lo.astype(F32)).astype(BF16)
    return hi, lo, lo2


def _dot(a, b):
    return jnp.dot(a, b, preferred_element_type=F32)


def _mm(a, b):
    return _dot(a.astype(BF16), b.astype(BF16))


def _mm_nt(a, b):
    return lax.dot_general(a.astype(BF16), b.astype(BF16), (((1,), (1,)), ((), ())),
                           preferred_element_type=F32)


def _mm_tn(a, b):
    return lax.dot_general(a.astype(BF16), b.astype(BF16), (((0,), (0,)), ((), ())),
                           preferred_element_type=F32)


def _mm_const_l(c, b):
    hi, lo, lo2 = _split3(b)
    return _dot(c, hi) + _dot(c, lo) + _dot(c, lo2)


def _mm_const_r(a, c):
    hi, lo, lo2 = _split3(a)
    n = a.shape[0]
    t = _dot(jnp.concatenate([hi, lo, lo2], axis=0), c)
    return t[0:n] + t[n:2 * n] + t[2 * n:3 * n]


def _layer_kernel(layer, n_chunks,
                  x_ref, gpre_ref, win_ref, wout_ref, gpost_ref, vec_ref, small_ref,
                  gconv_ref, sconvw_ref, sconvb_ref, mu_ref, wup_ref, aup_ref,
                  ltri_ref, bdm_ref, grpm_ref, eexp_ref, masks_ref,
                  o_ref,
                  p_ref, y_ref, s_hg, s_gdn, s_ssm, s_rwkv):
    t_idx = pl.program_id(1)
    tl = n_chunks * CHUNK

    @pl.when(t_idx == 0)
    def _():
        p_ref[0:TAIL, :] = jnp.zeros((TAIL, PROJ_COLS), F32)
        s_hg[...] = jnp.zeros_like(s_hg)
        s_gdn[...] = jnp.zeros_like(s_gdn)
        s_ssm[...] = jnp.zeros_like(s_ssm)
        s_rwkv[...] = jnp.zeros_like(s_rwkv)

    @pl.when(t_idx != 0)
    def _():
        p_ref[0:TAIL, :] = p_ref[tl:tl + TAIL, :]

    h = x_ref[...]
    u = h * lax.rsqrt(jnp.mean(h * h, axis=-1, keepdims=True) + NORM_EPS) * gpre_ref[...]
    p_ref[TAIL:TAIL + tl, :] = _dot(u.astype(BF16), win_ref[...])

    sub_r = lax.broadcasted_iota(jnp.int32, (SUB, GROUP_W), 0)
    row_c = lax.broadcasted_iota(jnp.int32, (CHUNK, GROUP_W), 0)
    lane_s = lax.broadcasted_iota(jnp.int32, (1, SMALL_W), 1)

    def vec(i):
        return vec_ref[i:i + 1, :]

    def mask(i):
        return masks_ref[i]

    def bd(y):
        return jnp.tile(y.astype(BF16), (N_HEADS, 1)) * bdm_ref[...]

    def head_sum(v):
        return _mm_const_r(v, bdm_ref[...])

    def mm3_bd(l, r):
        lh, ll = _split(l)
        rh, rl = _split(r)
        n = l.shape[0]
        t = _dot(jnp.concatenate([lh, ll], axis=0), bd(rh))
        return t[0:n] + t[n:2 * n] + _dot(lh, bd(rl))

    def tri_inverse_pair(n1, n2):
        eye = mask(M_EYE)
        x1 = eye - mask(M_LEVEL0) * n1
        x2 = eye - mask(M_LEVEL0) * n2
        for lev in range(1, N_LEVELS):
            m = mask(M_LEVEL0 + lev)
            w1 = mm3_bd(m * n1, x1)
            w2 = mm3_bd(m * n2, x2)
            x1 = x1 - mm3_bd(x1, w1)
            x2 = x2 - mm3_bd(x2, w2)
        return x1, x2

    depth = vec_ref.shape[0] - V_LB
    lbs = [vec(V_LB + l) for l in range(depth)]
    lb_max = functools.reduce(jnp.maximum, lbs)
    lb_exp = [jnp.exp(v - lb_max) for v in lbs]
    lb_den = functools.reduce(lambda a, b: a + b, lb_exp)
    sm = [e / lb_den for e in lb_exp]
    lb = functools.reduce(lambda a, b: a + b, sm[:layer + 1]) - sm[0]

    small_bias = small_ref[0:1, :]
    plain = small_ref[2:3, :]
    small_mult = plain - (1.0 - plain) * jnp.exp(small_ref[1:2, :])

    def chunk_body(c, carry):
        r0 = pl.multiple_of(c * CHUNK, CHUNK)
        cur = pl.ds(r0 + TAIL, CHUNK)
        win = pl.ds(r0, CHUNK + TAIL)
        causal = mask(M_CAUSAL)
        strict = mask(M_STRICT)

        gates = p_ref[cur, GATE_OFF:GATE_OFF + 4 * GROUP_W]

        small = p_ref[cur, SMALL_OFF:SMALL_OFF + SMALL_W]
        small_act = jnp.where(lane_s < N_HEADS, jax.nn.sigmoid(small),
                              jax.nn.softplus(small + small_bias) * small_mult)
        expd = _mm_const_r(small_act, eexp_ref[...])
        beta_e = expd[:, 0:GROUP_W]
        g_gdn = expd[:, GROUP_W:2 * GROUP_W]
        dt_e = expd[:, 2 * GROUP_W:3 * GROUP_W]
        g_ssd = expd[:, 3 * GROUP_W:4 * GROUP_W]

        hg = p_ref[cur, HG_OFF:HG_OFF + 3 * GROUP_W]
        q_a = _silu(hg[:, 0:GROUP_W]) * (HEAD_DIM ** -0.5)
        f_a = lb + (1.0 - lb) * jax.nn.sigmoid(hg[:, GROUP_W:2 * GROUP_W])
        k_a = 1.0 - f_a
        v_a = hg[:, 2 * GROUP_W:3 * GROUP_W]

        xw = p_ref[win, GDN_OFF:GDN_OFF + 3 * GROUP_W]
        xc = gconv_ref[0:1, :] * xw[TAIL - 3:TAIL - 3 + CHUNK, :]
        for j in range(1, CONV_K):
            xc = xc + gconv_ref[j:j + 1, :] * xw[TAIL - 3 + j:TAIL - 3 + j + CHUNK, :]
        xc = _silu(xc)
        q_b = xc[:, 0:GROUP_W]
        k_b = xc[:, GROUP_W:2 * GROUP_W]
        v_b = xc[:, 2 * GROUP_W:3 * GROUP_W]

        xw = p_ref[win, SSM_OFF:SSM_OFF + 3 * GROUP_W]
        xc = sconvw_ref[0:1, :] * xw[TAIL - 3:TAIL - 3 + CHUNK, :]
        for j in range(1, CONV_K):
            xc = xc + sconvw_ref[j:j + 1, :] * xw[TAIL - 3 + j:TAIL - 3 + j + CHUNK, :]
        xc = _silu(xc + sconvb_ref[...])
        x_c = xc[:, 0:GROUP_W]
        b_c = xc[:, GROUP_W:2 * GROUP_W]
        c_c = xc[:, 2 * GROUP_W:3 * GROUP_W]

        xw = p_ref[win, RWKV_OFF:RWKV_OFF + RWKV_W]
        xcur = xw[TAIL:TAIL + CHUNK, :]
        xr = xcur + (xw[TAIL - 1:TAIL - 1 + CHUNK, :] - xcur) * mu_ref[...]
        r_d = xr[:, 0:GROUP_W]
        k_d = xr[:, GROUP_W:2 * GROUP_W]
        v_d = xr[:, 2 * GROUP_W:3 * GROUP_W]
        wd = xr[:, 3 * GROUP_W:3 * GROUP_W + LOW_RANK]
        ad = xr[:, 3 * GROUP_W + LOW_RANK:3 * GROUP_W + 2 * LOW_RANK]
        w_d = -jax.nn.softplus(-(vec(V_W0) + _mm(jnp.tanh(wd), wup_ref[...]))) - 0.5
        lw = -jnp.exp(w_d)
        a_d = jax.nn.sigmoid(vec(V_A0) + _mm(ad, aup_ref[...]))
        kk = k_d * vec(V_KK)
        k_d = k_d * (1.0 + (a_d - 1.0) * vec(V_KA))

        sq = head_sum(jnp.concatenate([q_b * q_b, k_b * k_b, kk * kk], axis=0))
        q_b = q_b * lax.rsqrt(sq[0:CHUNK] + L2_EPS) * (HEAD_DIM ** -0.5)
        k_b = k_b * lax.rsqrt(sq[CHUNK:2 * CHUNK] + L2_EPS)
        kk = kk * lax.rsqrt(sq[2 * CHUNK:3 * CHUNK] + L2_EPS)

        cums = _mm_const_l(ltri_ref[...], jnp.concatenate(
            [jnp.log(f_a), lw, g_gdn, g_ssd, g_gdn * strict, g_ssd * strict], axis=1))
        cum_a = cums[:, 0:GROUP_W]
        cum_d = cums[:, GROUP_W:2 * GROUP_W]
        cum_b = cums[:, 2 * GROUP_W:3 * GROUP_W]
        cum_c = cums[:, 3 * GROUP_W:4 * GROUP_W]
        gam_b = causal * jnp.exp(jnp.minimum(cums[:, 4 * GROUP_W:5 * GROUP_W], 0.0))
        gam_c = causal * jnp.exp(jnp.minimum(cums[:, 5 * GROUP_W:6 * GROUP_W], 0.0))

        kb = k_b * beta_e
        kq = _mm_nt(jnp.concatenate([kb, q_b], axis=0), bd(k_b))
        n_b = strict * kq[0:CHUNK] * gam_b
        qk_b = causal * kq[CHUNK:2 * CHUNK] * gam_b

        p_inc = jnp.exp(cum_d)
        p_inv = jnp.exp(-cum_d)
        p_last = p_inc[CHUNK - 1:CHUNK, :]
        a_t = -kk * jnp.exp(cum_d - lw)
        b_t = kk * a_d * p_inv
        k_t = k_d * p_inv
        r_t = r_d * p_inc
        ar = jnp.concatenate([a_t, r_t], axis=0)
        arb = _mm_nt(ar, bd(b_t))
        ark = _mm_nt(ar, bd(k_t))
        ab = strict * arb[0:CHUNK]
        rb = causal * arb[CHUNK:2 * CHUNK]
        ak = strict * ark[0:CHUNK]
        rk = causal * ark[CHUNK:2 * CHUNK]

        x_b, x_d = tri_inverse_pair(n_b, -ab)

        cum_last = cum_a[CHUNK - 1:CHUNK, :]
        q_in = q_a * jnp.exp(cum_a)
        k_out = k_a * jnp.exp(cum_last - cum_a)
        o_diag = []
        a_rows = [jnp.zeros((SUB, GROUP_W), F32)]
        for i in range(CHUNK // SUB):
            lo = i * SUB
            cum_i = cum_a[lo:lo + SUB, :]
            q_i = q_a[lo:lo + SUB, :]
            zs = []
            for sl in range(SUB):
                s = lo + sl
                e = jnp.exp(jnp.minimum(cum_i - cum_a[s:s + 1, :], 0.0))
                zs.append(q_i * e * k_a[s:s + 1, :])
            coef = _mm(jnp.concatenate(zs, axis=0), bdm_ref[...])
            o_i = jnp.zeros((SUB, GROUP_W), F32)
            for sl in range(SUB):
                s = lo + sl
                cf = jnp.where(sub_r >= sl, coef[sl * SUB:(sl + 1) * SUB, :], 0.0)
                o_i = o_i + cf * v_a[s:s + 1, :]
            o_diag.append(o_i)
            if i > 0:
                cs = cum_a[lo - 1:lo, :]
                qd = q_i * jnp.exp(cum_i - cs)
                kh = jnp.where(row_c < lo, k_a * jnp.exp(jnp.minimum(cs - cum_a, 0.0)), 0.0)
                a_rows.append(_mm_nt(qd, bd(kh)))
        st_a = s_hg[...]
        o_a = (jnp.concatenate(o_diag, axis=0) + _mm(jnp.concatenate(a_rows, axis=0), bd(v_a))
               + _mm_nt(q_in, st_a))
        s_hg[...] = st_a * jnp.exp(cum_last) + bdm_ref[...].astype(F32) * _mm_tn(v_a, k_out)

        grpm = grpm_ref[...]
        cb = _mm_nt(c_c, jnp.tile(b_c.astype(BF16), (N_HEADS, 1)) * grpm)
        xv = x_c * dt_e
        c_last = cum_c[CHUNK - 1:CHUNK, :]
        st_c = s_ssm[...]
        o_c = _mm(cb * gam_c, bd(xv)) + jnp.exp(cum_c) * _mm(c_c, st_c)
        s_ssm[...] = st_c * jnp.exp(c_last) + grpm.astype(F32) * _mm_tn(b_c, xv * jnp.exp(c_last - cum_c))

        ecum_b = jnp.exp(cum_b)
        b_last = cum_b[CHUNK - 1:CHUNK, :]
        st_b = s_gdn[...]
        st_d = s_rwkv[...]
        uu = mm3_bd(x_b, v_b * beta_e)
        ww = mm3_bd(x_b, kb * ecum_b)
        rhs_d = _mm_nt(ar, st_d)
        wq = _mm(jnp.concatenate([ww, q_b * ecum_b], axis=0), st_b)
        v_new = uu - wq[0:CHUNK]
        u_d = mm3_bd(x_d, rhs_d[0:CHUNK] + _mm(ak, bd(v_d)))
        o_b = wq[CHUNK:2 * CHUNK] + _mm(qk_b, bd(v_new))
        o_d = rhs_d[CHUNK:2 * CHUNK] + _mm(rb, bd(u_d)) + _mm(rk, bd(v_d))
        bdm = bdm_ref[...].astype(F32)
        s_gdn[...] = st_b * jnp.exp(b_last) + bdm * _mm_tn(k_b * jnp.exp(b_last - cum_b), v_new)
        s_rwkv[...] = st_d * p_last + bdm * (_mm_tn(u_d, b_t * p_last) + _mm_tn(v_d, k_t * p_last))

        sums = head_sum(jnp.concatenate([o_a * o_a, o_b * o_b, o_d, r_d * k_d * vec(V_RK)], axis=0))
        y_a = o_a * lax.rsqrt(sums[0:CHUNK] * (1.0 / HEAD_DIM) + NORM_EPS) * vec(V_HG_G)
        y_b = o_b * lax.rsqrt(sums[CHUNK:2 * CHUNK] * (1.0 / HEAD_DIM) + NORM_EPS) * vec(V_GDN_G)
        oc = o_d - sums[2 * CHUNK:3 * CHUNK] * (1.0 / HEAD_DIM)
        var = head_sum(oc * oc) * (1.0 / HEAD_DIM)
        y_d = oc * lax.rsqrt(var + RWKV_GN_EPS) * vec(V_LNW) + vec(V_LNB) + sums[3 * CHUNK:4 * CHUNK] * v_d
        rows = pl.ds(r0, CHUNK)
        y_ref[rows, 0:GROUP_W] = (y_a * _silu(gates[:, 0:GROUP_W])).astype(y_ref.dtype)
        y_ref[rows, GROUP_W:2 * GROUP_W] = (y_b * _silu(gates[:, GROUP_W:2 * GROUP_W])).astype(y_ref.dtype)
        yv = (o_c + x_c * vec(V_SSM_D)) * _silu(gates[:, 2 * GROUP_W:3 * GROUP_W])
        gw = GROUP_W // SSM_GROUPS
        for grp in range(SSM_GROUPS):
            yg = yv[:, grp * gw:(grp + 1) * gw]
            yg = yg * lax.rsqrt(jnp.mean(yg * yg, axis=-1, keepdims=True) + NORM_EPS)
            yg = yg * vec_ref[V_SSM_G:V_SSM_G + 1, grp * gw:(grp + 1) * gw]
            y_ref[rows, 2 * GROUP_W + grp * gw:2 * GROUP_W + (grp + 1) * gw] = yg.astype(y_ref.dtype)
        y_ref[rows, 3 * GROUP_W:4 * GROUP_W] = (y_d * _silu(gates[:, 3 * GROUP_W:4 * GROUP_W])).astype(y_ref.dtype)
        return carry

    lax.fori_loop(0, n_chunks, chunk_body, 0)

    out = _dot(y_ref[...], wout_ref[...])
    out = out * lax.rsqrt(jnp.mean(out * out, axis=-1, keepdims=True) + NORM_EPS) * gpost_ref[...]
    o_ref[...] = x_ref[...] + out


def _layer_call(layer, h, *params):
    b, l, d = h.shape
    tl = min(TIME_TILE, l)
    assert l % tl == 0 and tl % CHUNK == 0 and d == D_MODEL
    n_chunks = tl // CHUNK
    bd_state = pltpu.VMEM((N_HEADS * HEAD_DIM, N_HEADS * HEAD_DIM), F32)

    def full(a):
        return pl.BlockSpec(a.shape, lambda i, j: (0,) * a.ndim)

    return pl.pallas_call(
        functools.partial(_layer_kernel, layer, n_chunks),
        grid=(b, l // tl),
        in_specs=[pl.BlockSpec((None, tl, d), lambda i, j: (i, j, 0))] + [full(a) for a in params],
        out_specs=pl.BlockSpec((None, tl, d), lambda i, j: (i, j, 0)),
        out_shape=jax.ShapeDtypeStruct(h.shape, h.dtype),
        scratch_shapes=[
            pltpu.VMEM((tl + TAIL, PROJ_COLS), F32),
            pltpu.VMEM((tl, D_MODEL), BF16),
            bd_state,
            bd_state,
            bd_state,
            bd_state,
        ],
        compiler_params=pltpu.CompilerParams(
            dimension_semantics=("arbitrary", "arbitrary"),
            vmem_limit_bytes=V7X_VMEM_LIMIT_BYTES),
        name=f"hybrid_layer{layer}",
    )(h, *params)


def _repack_w_in(w):
    hg = w[:, 0:768]
    gdn_qkv = w[:, 768:1536]
    gdn_small = w[:, 1536:1544]
    ssm_xbc = w[:, 1544:2312]
    ssm_dt = w[:, 2312:2316]
    rwkv = w[:, 2316:3212]
    gates = w[:, 3212:4236]
    pad = jnp.zeros((w.shape[0], SMALL_W - 4 * N_HEADS), w.dtype)
    return jnp.concatenate([hg, gdn_qkv, ssm_xbc, rwkv, gates, gdn_small, ssm_dt, ssm_dt, pad], axis=1).astype(BF16)


def _constants():
    ri = lax.broadcasted_iota(jnp.int32, (CHUNK, CHUNK), 0)
    ci = lax.broadcasted_iota(jnp.int32, (CHUNK, CHUNK), 1)
    ltri = (ri >= ci).astype(BF16)
    r2 = lax.broadcasted_iota(jnp.int32, (GROUP_W, GROUP_W), 0)
    c2 = lax.broadcasted_iota(jnp.int32, (GROUP_W, GROUP_W), 1)
    bdm = ((r2 // HEAD_DIM) == (c2 // HEAD_DIM)).astype(BF16)
    gw = GROUP_W // SSM_GROUPS
    grpm = ((r2 // gw) == (c2 // gw)).astype(BF16)
    er = lax.broadcasted_iota(jnp.int32, (SMALL_W, 4 * GROUP_W), 0)
    ec = lax.broadcasted_iota(jnp.int32, (SMALL_W, 4 * GROUP_W), 1)
    eexp = (er == ec // HEAD_DIM).astype(BF16)
    levels = []
    for lev in range(N_LEVELS):
        s = 2 ** lev
        levels.append(((ri // (2 * s)) == (ci // (2 * s))) & (((ri // s) % 2) == 1) & (((ci // s) % 2) == 0))
    masks = jnp.stack([jnp.tile(m.astype(F32), (1, N_HEADS)) for m in [ri >= ci, ri > ci, ri == ci] + levels])
    return ltri, bdm, grpm, eexp, masks


def kernel(x, pre_norm_g, w_in, hgrn_lower_bounds, hgrn_norm_g, gdn_conv_w, gdn_a_log, gdn_dt_bias, gdn_norm_g,
           ssm_conv_w, ssm_conv_b, ssm_a_log, ssm_dt_bias, ssm_d, ssm_norm_g, rwkv_mu, rwkv_w0, rwkv_w_up,
           rwkv_a0, rwkv_a_up, rwkv_k_k, rwkv_k_a, rwkv_r_k, rwkv_ln_w, rwkv_ln_b, w_out, post_norm_g):
    depth = w_in.shape[0]
    consts = _constants()
    zeros4 = jnp.zeros((N_HEADS,), F32)
    ones4 = jnp.ones((N_HEADS,), F32)
    pad = jnp.zeros((SMALL_W - 4 * N_HEADS,), F32)
    h = x
    for l in range(depth):
        vecs = jnp.stack([
            jnp.tile(hgrn_norm_g[l], N_HEADS), jnp.tile(gdn_norm_g[l], N_HEADS),
            jnp.repeat(ssm_d[l], HEAD_DIM), ssm_norm_g[l], rwkv_w0[l], rwkv_a0[l], rwkv_k_k[l],
            rwkv_k_a[l], rwkv_r_k[l], rwkv_ln_w[l], rwkv_ln_b[l]] + [hgrn_lower_bounds[i] for i in range(depth)])
        small = jnp.stack([
            jnp.concatenate([zeros4, gdn_dt_bias[l], ssm_dt_bias[l], ssm_dt_bias[l], pad]),
            jnp.concatenate([zeros4, gdn_a_log[l], zeros4, ssm_a_log[l], pad]),
            jnp.concatenate([zeros4, zeros4, ones4, zeros4, pad])])
        h = _layer_call(
            l, h, pre_norm_g[l][None, :], _repack_w_in(w_in[l]), w_out[l].astype(BF16), post_norm_g[l][None, :],
            vecs.astype(F32), small.astype(F32), gdn_conv_w[l], ssm_conv_w[l], ssm_conv_b[l][None, :],
            rwkv_mu[l][None, :], rwkv_w_up[l].astype(BF16), rwkv_a_up[l].astype(BF16), *consts)
    return h
```

```python
import functools

import jax
import jax.numpy as jnp
from jax import lax
from jax.experimental import pallas as pl
from jax.experimental.pallas import tpu as pltpu

F32 = jnp.float32
BF16 = jnp.bfloat16

D_MODEL = 1024
GROUP_W = 256
HEAD_DIM = 64
N_HEADS = 4
CHUNK = 64
SUB = 16
HALF = SUB // 2
CONV_K = 4
SSM_STATE = 128
SSM_GROUPS = 2
LOW_RANK = 64
NORM_EPS = 1e-6
L2_EPS = 1e-6
RWKV_GN_EPS = 64e-5

HG_OFF = 0
GDN_OFF = 768
SSM_OFF = 1536
RWKV_OFF = 2304
RWKV_W = 3 * GROUP_W + 2 * LOW_RANK
GATE_OFF = 3200
SMALL_OFF = 4224
SMALL_W = 128
PROJ_COLS = 4352

TAIL = 8
TIME_TILE = 256
V7X_VMEM_LIMIT_BYTES = 52 * 1024 * 1024

(V_HG_G, V_GDN_G, V_SSM_D, V_SSM_G, V_W0, V_A0, V_KK, V_KA, V_RK, V_LNW, V_LNB) = range(11)
V_LB = 11

M_CAUSAL, M_STRICT, M_EYE, M_LEVEL0 = 0, 1, 2, 3
N_LEVELS = 6
N_BD_SLOTS = 36


def _silu(x):
    return x * jax.nn.sigmoid(x)


def _split(a):
    hi = a.astype(BF16)
    lo = (a - hi.astype(F32)).astype(BF16)
    return hi, lo


def _dot(a, b):
    return jnp.dot(a, b, preferred_element_type=F32)


def _mm(a, b):
    return _dot(a.astype(BF16), b.astype(BF16))


def _mm_nt(a, b):
    return lax.dot_general(a.astype(BF16), b.astype(BF16), (((1,), (1,)), ((), ())),
                           preferred_element_type=F32)


def _mm_tn(a, b):
    return lax.dot_general(a.astype(BF16), b.astype(BF16), (((0,), (0,)), ((), ())),
                           preferred_element_type=F32)


def _mm_const_l(c, b):
    hi, lo = _split(b)
    return _dot(c, hi) + _dot(c, lo)


def _mm_const_r(a, c):
    hi, lo = _split(a)
    n = a.shape[0]
    t = _dot(jnp.concatenate([hi, lo], axis=0), c)
    return t[0:n] + t[n:2 * n]


def _layer_kernel(layer, n_chunks,
                  x_ref, gpre_ref, win0_ref, win1_ref, win2_ref, win3_ref, win4_ref,
                  wout_ref, gpost_ref, vec_ref, small_ref,
                  gconv_ref, sconvw_ref, sconvb_ref, mu_ref, wup_ref, aup_ref,
                  ltri_ref, bdm_ref, bdmf_ref, grpmf_ref, eexp_ref, masks_ref,
                  o_ref,
                  p_ref, y_ref, bd_ref, s_hg, s_gdn, s_ssm, s_rwkv):
    t_idx = pl.program_id(1)
    tl = n_chunks * CHUNK
    win_refs = (win0_ref, win1_ref, win2_ref, win3_ref, win4_ref)
    assert sum(w.shape[1] for w in win_refs) == PROJ_COLS

    @pl.when(t_idx == 0)
    def _():
        p_ref[0:TAIL, :] = jnp.zeros((TAIL, PROJ_COLS), F32)
        bd_ref[...] = jnp.zeros_like(bd_ref)
        s_hg[...] = jnp.zeros_like(s_hg)
        s_gdn[...] = jnp.zeros_like(s_gdn)
        s_ssm[...] = jnp.zeros_like(s_ssm)
        s_rwkv[...] = jnp.zeros_like(s_rwkv)

    @pl.when(t_idx != 0)
    def _():
        p_ref[0:TAIL, :] = p_ref[tl:tl + TAIL, :]

    h = x_ref[...]
    u = h * lax.rsqrt(jnp.mean(h * h, axis=-1, keepdims=True) + NORM_EPS) * gpre_ref[...]
    ub = u.astype(BF16)
    col = 0
    for w_ref in win_refs:
        p_ref[TAIL:TAIL + tl, col:col + w_ref.shape[1]] = _dot(ub, w_ref[...])
        col += w_ref.shape[1]

    half_r = lax.broadcasted_iota(jnp.int32, (HALF, GROUP_W), 0)
    row_c = lax.broadcasted_iota(jnp.int32, (CHUNK, GROUP_W), 0)
    lane_s = lax.broadcasted_iota(jnp.int32, (1, SMALL_W), 1)

    def vec(i):
        return vec_ref[i:i + 1, :]

    def mask(i):
        return masks_ref[i]

    slots = iter(range(N_BD_SLOTS))

    def bd(y, width=HEAD_DIM):
        slot = next(slots)
        yb = y.astype(BF16)
        for hd in range(N_HEADS):
            lo = (hd * HEAD_DIM // width) * width
            bd_ref[slot, hd * CHUNK:(hd + 1) * CHUNK, lo:lo + width] = yb[:, lo:lo + width]
        return bd_ref[slot]

    def head_sum(v):
        return _mm(v, bdm_ref[...])

    def mm_bd(l, r):
        return _mm(l, bd(r))

    def tri_inverse_pair(n1, n2):
        eye = mask(M_EYE)
        x1 = eye - mask(M_LEVEL0) * n1
        x2 = eye - mask(M_LEVEL0) * n2
        for lev in range(1, N_LEVELS):
            m = mask(M_LEVEL0 + lev)
            w1 = mm_bd(m * n1, x1)
            w2 = mm_bd(m * n2, x2)
            x1 = x1 - mm_bd(x1, w1)
            x2 = x2 - mm_bd(x2, w2)
        return x1, x2

    depth = vec_ref.shape[0] - V_LB
    lbs = [vec(V_LB + l) for l in range(depth)]
    lb_max = functools.reduce(jnp.maximum, lbs)
    lb_exp = [jnp.exp(v - lb_max) for v in lbs]
    lb_den = functools.reduce(lambda a, b: a + b, lb_exp)
    sm = [e / lb_den for e in lb_exp]
    lb = functools.reduce(lambda a, b: a + b, sm[:layer + 1]) - sm[0]

    small_bias = small_ref[0:1, :]
    plain = small_ref[2:3, :]
    small_mult = plain - (1.0 - plain) * jnp.exp(small_ref[1:2, :])

    def chunk_body(c, carry):
        r0 = pl.multiple_of(c * CHUNK, CHUNK)
        cur = pl.ds(r0 + TAIL, CHUNK)
        win = pl.ds(r0, CHUNK + TAIL)
        causal = mask(M_CAUSAL)
        strict = mask(M_STRICT)

        gates = p_ref[cur, GATE_OFF:GATE_OFF + 4 * GROUP_W]

        small = p_ref[cur, SMALL_OFF:SMALL_OFF + SMALL_W]
        small_act = jnp.where(lane_s < N_HEADS, jax.nn.sigmoid(small),
                              jax.nn.softplus(small + small_bias) * small_mult)
        expd = _mm_const_r(small_act, eexp_ref[...])
        beta_e = expd[:, 0:GROUP_W]
        g_gdn = expd[:, GROUP_W:2 * GROUP_W]
        dt_e = expd[:, 2 * GROUP_W:3 * GROUP_W]
        g_ssd = expd[:, 3 * GROUP_W:4 * GROUP_W]

        hg = p_ref[cur, HG_OFF:HG_OFF + 3 * GROUP_W]
        q_a = _silu(hg[:, 0:GROUP_W]) * (HEAD_DIM ** -0.5)
        f_a = lb + (1.0 - lb) * jax.nn.sigmoid(hg[:, GROUP_W:2 * GROUP_W])
        k_a = 1.0 - f_a
        v_a = hg[:, 2 * GROUP_W:3 * GROUP_W]

        xw = p_ref[win, GDN_OFF:GDN_OFF + 3 * GROUP_W]
        xc = gconv_ref[0:1, :] * xw[TAIL - 3:TAIL - 3 + CHUNK, :]
        for j in range(1, CONV_K):
            xc = xc + gconv_ref[j:j + 1, :] * xw[TAIL - 3 + j:TAIL - 3 + j + CHUNK, :]
        xc = _silu(xc)
        q_b = xc[:, 0:GROUP_W]
        k_b = xc[:, GROUP_W:2 * GROUP_W]
        v_b = xc[:, 2 * GROUP_W:3 * GROUP_W]

        xw = p_ref[win, SSM_OFF:SSM_OFF + 3 * GROUP_W]
        xc = sconvw_ref[0:1, :] * xw[TAIL - 3:TAIL - 3 + CHUNK, :]
        for j in range(1, CONV_K):
            xc = xc + sconvw_ref[j:j + 1, :] * xw[TAIL - 3 + j:TAIL - 3 + j + CHUNK, :]
        xc = _silu(xc + sconvb_ref[...])
        x_c = xc[:, 0:GROUP_W]
        b_c = xc[:, GROUP_W:2 * GROUP_W]
        c_c = xc[:, 2 * GROUP_W:3 * GROUP_W]

        xw = p_ref[win, RWKV_OFF:RWKV_OFF + RWKV_W]
        xcur = xw[TAIL:TAIL + CHUNK, :]
        xr = xcur + (xw[TAIL - 1:TAIL - 1 + CHUNK, :] - xcur) * mu_ref[...]
        r_d = xr[:, 0:GROUP_W]
        k_d = xr[:, GROUP_W:2 * GROUP_W]
        v_d = xr[:, 2 * GROUP_W:3 * GROUP_W]
        wd = xr[:, 3 * GROUP_W:3 * GROUP_W + LOW_RANK]
        ad = xr[:, 3 * GROUP_W + LOW_RANK:3 * GROUP_W + 2 * LOW_RANK]
        w_d = -jax.nn.softplus(-(vec(V_W0) + _mm(jnp.tanh(wd), wup_ref[...]))) - 0.5
        lw = -jnp.exp(w_d)
        a_d = jax.nn.sigmoid(vec(V_A0) + _mm(ad, aup_ref[...]))
        kk = k_d * vec(V_KK)
        k_d = k_d * (1.0 + (a_d - 1.0) * vec(V_KA))

        sq = head_sum(jnp.concatenate([q_b * q_b, k_b * k_b, kk * kk], axis=0))
        q_b = q_b * lax.rsqrt(sq[0:CHUNK] + L2_EPS) * (HEAD_DIM ** -0.5)
        k_b = k_b * lax.rsqrt(sq[CHUNK:2 * CHUNK] + L2_EPS)
        kk = kk * lax.rsqrt(sq[2 * CHUNK:3 * CHUNK] + L2_EPS)

        cums = _mm_const_l(ltri_ref[...], jnp.concatenate(
            [jnp.log(f_a), lw, g_gdn, g_ssd, g_gdn * strict, g_ssd * strict], axis=1))
        cum_a = cums[:, 0:GROUP_W]
        cum_d = cums[:, GROUP_W:2 * GROUP_W]
        cum_b = cums[:, 2 * GROUP_W:3 * GROUP_W]
        cum_c = cums[:, 3 * GROUP_W:4 * GROUP_W]
        gam_b = causal * jnp.exp(jnp.minimum(cums[:, 4 * GROUP_W:5 * GROUP_W], 0.0))
        gam_c = causal * jnp.exp(jnp.minimum(cums[:, 5 * GROUP_W:6 * GROUP_W], 0.0))

        kb = k_b * beta_e
        kq = _mm_nt(jnp.concatenate([kb, q_b], axis=0), bd(k_b))
        n_b = strict * kq[0:CHUNK] * gam_b
        qk_b = causal * kq[CHUNK:2 * CHUNK] * gam_b

        p_inc = jnp.exp(cum_d)
        p_inv = jnp.exp(-cum_d)
        p_last = p_inc[CHUNK - 1:CHUNK, :]
        a_t = -kk * jnp.exp(cum_d - lw)
        b_t = kk * a_d * p_inv
        k_t = k_d * p_inv
        r_t = r_d * p_inc
        ar = jnp.concatenate([a_t, r_t], axis=0)
        arb = _mm_nt(ar, bd(b_t))
        ark = _mm_nt(ar, bd(k_t))
        ab = strict * arb[0:CHUNK]
        rb = causal * arb[CHUNK:2 * CHUNK]
        ak = strict * ark[0:CHUNK]
        rk = causal * ark[CHUNK:2 * CHUNK]

        x_b, x_d = tri_inverse_pair(n_b, -ab)

        cum_last = cum_a[CHUNK - 1:CHUNK, :]
        q_in = q_a * jnp.exp(cum_a)
        k_out = k_a * jnp.exp(cum_last - cum_a)
        o_diag = []
        a_rows = [jnp.zeros((SUB, GROUP_W), F32)]
        for i in range(CHUNK // SUB):
            lo = i * SUB
            cum_i = cum_a[lo:lo + SUB, :]
            q_i = q_a[lo:lo + SUB, :]
            zs = []
            for sl in range(SUB):
                s = lo + sl
                t0 = 0 if sl < HALF else HALF
                e = jnp.exp(jnp.minimum(cum_i[t0:SUB, :] - cum_a[s:s + 1, :], 0.0))
                zs.append(q_i[t0:SUB, :] * e * k_a[s:s + 1, :])
            coef = _mm(jnp.concatenate(zs, axis=0), bdm_ref[...])
            o_top = jnp.zeros((HALF, GROUP_W), F32)
            o_bot = jnp.zeros((HALF, GROUP_W), F32)
            row = 0
            for sl in range(SUB):
                v_s = v_a[lo + sl:lo + sl + 1, :]
                if sl < HALF:
                    o_top = o_top + jnp.where(half_r >= sl, coef[row:row + HALF, :], 0.0) * v_s
                    o_bot = o_bot + coef[row + HALF:row + SUB, :] * v_s
                    row += SUB
                else:
                    o_bot = o_bot + jnp.where(half_r >= sl - HALF, coef[row:row + HALF, :], 0.0) * v_s
                    row += HALF
            o_diag += [o_top, o_bot]
            if i > 0:
                cs = cum_a[lo - 1:lo, :]
                qd = q_i * jnp.exp(cum_i - cs)
                kh = jnp.where(row_c < lo, k_a * jnp.exp(jnp.minimum(cs - cum_a, 0.0)), 0.0)
                a_rows.append(_mm_nt(qd, bd(kh)))
        st_a = s_hg[...]
        o_a = (jnp.concatenate(o_diag, axis=0) + _mm(jnp.concatenate(a_rows, axis=0), bd(v_a))
               + _mm_nt(q_in, st_a))
        s_hg[...] = st_a * jnp.exp(cum_last) + bdmf_ref[...] * _mm_tn(v_a, k_out)

        cb = _mm_nt(c_c, bd(b_c, GROUP_W // SSM_GROUPS))
        xv = x_c * dt_e
        c_last = cum_c[CHUNK - 1:CHUNK, :]
        st_c = s_ssm[...]
        o_c = _mm(cb * gam_c, bd(xv)) + jnp.exp(cum_c) * _mm(c_c, st_c)
        s_ssm[...] = st_c * jnp.exp(c_last) + grpmf_ref[...] * _mm_tn(b_c, xv * jnp.exp(c_last - cum_c))

        ecum_b = jnp.exp(cum_b)
        b_last = cum_b[CHUNK - 1:CHUNK, :]
        st_b = s_gdn[...]
        st_d = s_rwkv[...]
        uu = mm_bd(x_b, v_b * beta_e)
        ww = mm_bd(x_b, kb * ecum_b)
        rhs_d = _mm_nt(ar, st_d)
        wq = _mm(jnp.concatenate([ww, q_b * ecum_b], axis=0), st_b)
        v_new = uu - wq[0:CHUNK]
        bd_vd = bd(v_d)
        u_d = mm_bd(x_d, rhs_d[0:CHUNK] + _mm(ak, bd_vd))
        o_b = wq[CHUNK:2 * CHUNK] + _mm(qk_b, bd(v_new))
        o_d = rhs_d[CHUNK:2 * CHUNK] + _mm(rb, bd(u_d)) + _mm(rk, bd_vd)
        bdm = bdmf_ref[...]
        s_gdn[...] = st_b * jnp.exp(b_last) + bdm * _mm_tn(k_b * jnp.exp(b_last - cum_b), v_new)
        s_rwkv[...] = st_d * p_last + bdm * (_mm_tn(u_d, b_t * p_last) + _mm_tn(v_d, k_t * p_last))

        sums = head_sum(jnp.concatenate([o_a * o_a, o_b * o_b, o_d, r_d * k_d * vec(V_RK)], axis=0))
        y_a = o_a * lax.rsqrt(sums[0:CHUNK] * (1.0 / HEAD_DIM) + NORM_EPS) * vec(V_HG_G)
        y_b = o_b * lax.rsqrt(sums[CHUNK:2 * CHUNK] * (1.0 / HEAD_DIM) + NORM_EPS) * vec(V_GDN_G)
        oc = o_d - sums[2 * CHUNK:3 * CHUNK] * (1.0 / HEAD_DIM)
        var = head_sum(oc * oc) * (1.0 / HEAD_DIM)
        y_d = oc * lax.rsqrt(var + RWKV_GN_EPS) * vec(V_LNW) + vec(V_LNB) + sums[3 * CHUNK:4 * CHUNK] * v_d
        rows = pl.ds(r0, CHUNK)
        y_ref[rows, 0:GROUP_W] = (y_a * _silu(gates[:, 0:GROUP_W])).astype(y_ref.dtype)
        y_ref[rows, GROUP_W:2 * GROUP_W] = (y_b * _silu(gates[:, GROUP_W:2 * GROUP_W])).astype(y_ref.dtype)
        yv = (o_c + x_c * vec(V_SSM_D)) * _silu(gates[:, 2 * GROUP_W:3 * GROUP_W])
        gw = GROUP_W // SSM_GROUPS
        for grp in range(SSM_GROUPS):
            yg = yv[:, grp * gw:(grp + 1) * gw]
            yg = yg * lax.rsqrt(jnp.mean(yg * yg, axis=-1, keepdims=True) + NORM_EPS)
            yg = yg * vec_ref[V_SSM_G:V_SSM_G + 1, grp * gw:(grp + 1) * gw]
            y_ref[rows, 2 * GROUP_W + grp * gw:2 * GROUP_W + (grp + 1) * gw] = yg.astype(y_ref.dtype)
        y_ref[rows, 3 * GROUP_W:4 * GROUP_W] = (y_d * _silu(gates[:, 3 * GROUP_W:4 * GROUP_W])).astype(y_ref.dtype)
        return carry

    lax.fori_loop(0, n_chunks, chunk_body, 0)

    out = _dot(y_ref[...], wout_ref[...])
    out = out * lax.rsqrt(jnp.mean(out * out, axis=-1, keepdims=True) + NORM_EPS) * gpost_ref[...]
    o_ref[...] = x_ref[...] + out


def _layer_call(layer, h, *params):
    b, l, d = h.shape
    tl = min(TIME_TILE, l)
    assert l % tl == 0 and tl % CHUNK == 0 and d == D_MODEL
    n_chunks = tl // CHUNK
    bd_state = pltpu.VMEM((N_HEADS * HEAD_DIM, N_HEADS * HEAD_DIM), F32)

    def full(a):
        return pl.BlockSpec(a.shape, lambda i, j: (0,) * a.ndim)

    return pl.pallas_call(
        functools.partial(_layer_kernel, layer, n_chunks),
        grid=(b, l // tl),
        in_specs=[pl.BlockSpec((None, tl, d), lambda i, j: (i, j, 0))] + [full(a) for a in params],
        out_specs=pl.BlockSpec((None, tl, d), lambda i, j: (i, j, 0)),
        out_shape=jax.ShapeDtypeStruct(h.shape, h.dtype),
        scratch_shapes=[
            pltpu.VMEM((tl + TAIL, PROJ_COLS), F32),
            pltpu.VMEM((tl, D_MODEL), BF16),
            pltpu.VMEM((N_BD_SLOTS, N_HEADS * CHUNK, GROUP_W), BF16),
            bd_state,
            bd_state,
            bd_state,
            bd_state,
        ],
        compiler_params=pltpu.CompilerParams(
            dimension_semantics=("arbitrary", "arbitrary"),
            vmem_limit_bytes=V7X_VMEM_LIMIT_BYTES),
        name=f"hybrid_layer{layer}",
    )(h, *params)


def _split_w_in(w):
    hg_gdn = w[:, 0:1536]
    gdn_small = w[:, 1536:1544]
    ssm_xbc = w[:, 1544:2312]
    ssm_dt = w[:, 2312:2316]
    rwkv = w[:, 2316:3212]
    gates = w[:, 3212:4236]
    pad = jnp.zeros((w.shape[0], SMALL_W - 4 * N_HEADS), w.dtype)
    small = jnp.concatenate([gdn_small, ssm_dt, ssm_dt, pad], axis=1)
    return tuple(p.astype(BF16) for p in (hg_gdn, ssm_xbc, rwkv, gates, small))


def _constants():
    ri = lax.broadcasted_iota(jnp.int32, (CHUNK, CHUNK), 0)
    ci = lax.broadcasted_iota(jnp.int32, (CHUNK, CHUNK), 1)
    ltri = (ri >= ci).astype(BF16)
    r2 = lax.broadcasted_iota(jnp.int32, (GROUP_W, GROUP_W), 0)
    c2 = lax.broadcasted_iota(jnp.int32, (GROUP_W, GROUP_W), 1)
    bdm = ((r2 // HEAD_DIM) == (c2 // HEAD_DIM)).astype(BF16)
    gw = GROUP_W // SSM_GROUPS
    grpm = ((r2 // gw) == (c2 // gw)).astype(F32)
    er = lax.broadcasted_iota(jnp.int32, (SMALL_W, 4 * GROUP_W), 0)
    ec = lax.broadcasted_iota(jnp.int32, (SMALL_W, 4 * GROUP_W), 1)
    eexp = (er == ec // HEAD_DIM).astype(BF16)
    levels = []
    for lev in range(N_LEVELS):
        s = 2 ** lev
        levels.append(((ri // (2 * s)) == (ci // (2 * s))) & (((ri // s) % 2) == 1) & (((ci // s) % 2) == 0))
    masks = jnp.stack([jnp.tile(m.astype(F32), (1, N_HEADS)) for m in [ri >= ci, ri > ci, ri == ci] + levels])
    return ltri, bdm, bdm.astype(F32), grpm, eexp, masks


def kernel(x, pre_norm_g, w_in, hgrn_lower_bounds, hgrn_norm_g, gdn_conv_w, gdn_a_log, gdn_dt_bias, gdn_norm_g,
           ssm_conv_w, ssm_conv_b, ssm_a_log, ssm_dt_bias, ssm_d, ssm_norm_g, rwkv_mu, rwkv_w0, rwkv_w_up,
           rwkv_a0, rwkv_a_up, rwkv_k_k, rwkv_k_a, rwkv_r_k, rwkv_ln_w, rwkv_ln_b, w_out, post_norm_g):
    depth = w_in.shape[0]
    consts = _constants()
    zeros4 = jnp.zeros((N_HEADS,), F32)
    ones4 = jnp.ones((N_HEADS,), F32)
    pad = jnp.zeros((SMALL_W - 4 * N_HEADS,), F32)
    h = x
    for l in range(depth):
        vecs = jnp.stack([
            jnp.tile(hgrn_norm_g[l], N_HEADS), jnp.tile(gdn_norm_g[l], N_HEADS),
            jnp.repeat(ssm_d[l], HEAD_DIM), ssm_norm_g[l], rwkv_w0[l], rwkv_a0[l], rwkv_k_k[l],
            rwkv_k_a[l], rwkv_r_k[l], rwkv_ln_w[l], rwkv_ln_b[l]] + [hgrn_lower_bounds[i] for i in range(depth)])
        small = jnp.stack([
            jnp.concatenate([zeros4, gdn_dt_bias[l], ssm_dt_bias[l], ssm_dt_bias[l], pad]),
            jnp.concatenate([zeros4, gdn_a_log[l], zeros4, ssm_a_log[l], pad]),
            jnp.concatenate([zeros4, zeros4, ones4, zeros4, pad])])
        h = _layer_call(
            l, h, pre_norm_g[l][None, :], *_split_w_in(w_in[l]), w_out[l].astype(BF16), post_norm_g[l][None, :],
            vecs.astype(F32), small.astype(F32), gdn_conv_w[l], ssm_conv_w[l], ssm_conv_b[l][None, :],
            rwkv_mu[l][None, :], rwkv_w_up[l].astype(BF16), rwkv_a_up[l].astype(BF16), *consts)
    return h
```

```python
import functools

import jax
import jax.numpy as jnp
from jax import lax
from jax.experimental import pallas as pl
from jax.experimental.pallas import tpu as pltpu

F32 = jnp.float32
BF16 = jnp.bfloat16

D_MODEL = 1024
GROUP_W = 256
HEAD_DIM = 64
N_HEADS = 4
CHUNK = 64
SUB = 16
HALF = SUB // 2
CONV_K = 4
SSM_STATE = 128
SSM_GROUPS = 2
LOW_RANK = 64
NORM_EPS = 1e-6
L2_EPS = 1e-6
RWKV_GN_EPS = 64e-5

HG_OFF = 0
GDN_OFF = 768
SSM_OFF = 1536
RWKV_OFF = 2304
RWKV_W = 3 * GROUP_W + 2 * LOW_RANK
GATE_OFF = 3200
SMALL_OFF = 4224
SMALL_W = 128
PROJ_COLS = 4352

TAIL = 8
TIME_TILE = 256
CAST_ROWS = 256
V7X_VMEM_LIMIT_BYTES = 52 * 1024 * 1024

(V_HG_G, V_GDN_G, V_SSM_D, V_SSM_G, V_W0, V_A0, V_KK, V_KA, V_RK, V_LNW, V_LNB) = range(11)
V_LB = 11

M_CAUSAL, M_STRICT, M_EYE, M_LEVEL0 = 0, 1, 2, 3
N_LEVELS = 6
LOCKSTEP = 2
N_BD_SLOTS = 36 * LOCKSTEP


def _silu(x):
    return x * jax.nn.sigmoid(x)


def _split(a):
    hi = a.astype(BF16)
    lo = (a - hi.astype(F32)).astype(BF16)
    return hi, lo


def _dot(a, b):
    return jnp.dot(a, b, preferred_element_type=F32)


def _mm(a, b):
    return _dot(a.astype(BF16), b.astype(BF16))


def _mm_nt(a, b):
    return lax.dot_general(a.astype(BF16), b.astype(BF16), (((1,), (1,)), ((), ())),
                           preferred_element_type=F32)


def _mm_tn(a, b):
    return lax.dot_general(a.astype(BF16), b.astype(BF16), (((0,), (0,)), ((), ())),
                           preferred_element_type=F32)


def _mm_const_l(c, b):
    hi, lo = _split(b)
    return _dot(c, hi) + _dot(c, lo)


def _mm_const_r(a, c):
    hi, lo = _split(a)
    n = a.shape[0]
    t = _dot(jnp.concatenate([hi, lo], axis=0), c)
    return t[0:n] + t[n:2 * n]


def _layer_kernel(layer, n_chunks,
                  x_ref, gpre_ref, win0_ref, win1_ref, win2_ref, win3_ref, win4_ref,
                  wout_ref, gpost_ref, vec_ref, small_ref,
                  gconv_ref, sconvw_ref, sconvb_ref, mu_ref, wup_ref, aup_ref,
                  ltri_ref, bdm_ref, bdmf_ref, grpmf_ref, eexp_ref, masks_ref,
                  o_ref,
                  p_ref, y_ref, bd_ref, s_hg, s_gdn, s_ssm, s_rwkv):
    t_idx = pl.program_id(1)
    tl = n_chunks * CHUNK
    win_refs = (win0_ref, win1_ref, win2_ref, win3_ref, win4_ref)
    assert sum(w.shape[1] for w in win_refs) == PROJ_COLS

    @pl.when(t_idx == 0)
    def _():
        p_ref[0:TAIL, :] = jnp.zeros((TAIL, PROJ_COLS), F32)
        bd_ref[...] = jnp.zeros_like(bd_ref)
        s_hg[...] = jnp.zeros_like(s_hg)
        s_gdn[...] = jnp.zeros_like(s_gdn)
        s_ssm[...] = jnp.zeros_like(s_ssm)
        s_rwkv[...] = jnp.zeros_like(s_rwkv)

    @pl.when(t_idx != 0)
    def _():
        p_ref[0:TAIL, :] = p_ref[tl:tl + TAIL, :]

    h = x_ref[...]
    u = h * lax.rsqrt(jnp.mean(h * h, axis=-1, keepdims=True) + NORM_EPS) * gpre_ref[...]
    ub = u.astype(BF16)
    col = 0
    for w_ref in win_refs:
        p_ref[TAIL:TAIL + tl, col:col + w_ref.shape[1]] = _dot(ub, w_ref[...])
        col += w_ref.shape[1]

    half_r = lax.broadcasted_iota(jnp.int32, (HALF, GROUP_W), 0)
    row_c = lax.broadcasted_iota(jnp.int32, (CHUNK, GROUP_W), 0)
    lane_s = lax.broadcasted_iota(jnp.int32, (1, SMALL_W), 1)

    def vec(i):
        return vec_ref[i:i + 1, :]

    def mask(i):
        return masks_ref[i]

    slots_used = []

    def bd(y, width=HEAD_DIM):
        slot = len(slots_used)
        assert slot < N_BD_SLOTS
        slots_used.append(slot)
        yb = y.astype(BF16)
        for hd in range(N_HEADS):
            lo = (hd * HEAD_DIM // width) * width
            bd_ref[slot, hd * CHUNK:(hd + 1) * CHUNK, lo:lo + width] = yb[:, lo:lo + width]
        return bd_ref[slot]

    def head_sum(v):
        return _mm(v, bdm_ref[...])

    def mm_bd(l, r):
        return _mm(l, bd(r))

    def tri_inverses(ns):
        eye = mask(M_EYE)
        xs = [eye - mask(M_LEVEL0) * n for n in ns]
        for lev in range(1, N_LEVELS):
            m = mask(M_LEVEL0 + lev)
            ws = [mm_bd(m * n, x) for n, x in zip(ns, xs)]
            xs = [x - mm_bd(x, w) for x, w in zip(xs, ws)]
        return xs

    depth = vec_ref.shape[0] - V_LB
    lbs = [vec(V_LB + l) for l in range(depth)]
    lb_max = functools.reduce(jnp.maximum, lbs)
    lb_exp = [jnp.exp(v - lb_max) for v in lbs]
    lb_den = functools.reduce(lambda a, b: a + b, lb_exp)
    sm = [e / lb_den for e in lb_exp]
    lb = functools.reduce(lambda a, b: a + b, sm[:layer + 1]) - sm[0]

    small_bias = small_ref[0:1, :]
    plain = small_ref[2:3, :]
    small_mult = plain - (1.0 - plain) * jnp.exp(small_ref[1:2, :])

    def phase_a(c):
        r0 = pl.multiple_of(c * CHUNK, CHUNK)
        cur = pl.ds(r0 + TAIL, CHUNK)
        win = pl.ds(r0, CHUNK + TAIL)
        causal = mask(M_CAUSAL)
        strict = mask(M_STRICT)

        gates = p_ref[cur, GATE_OFF:GATE_OFF + 4 * GROUP_W]

        small = p_ref[cur, SMALL_OFF:SMALL_OFF + SMALL_W]
        small_act = jnp.where(lane_s < N_HEADS, jax.nn.sigmoid(small),
                              jax.nn.softplus(small + small_bias) * small_mult)
        expd = _mm_const_r(small_act, eexp_ref[...])
        beta_e = expd[:, 0:GROUP_W]
        g_gdn = expd[:, GROUP_W:2 * GROUP_W]
        dt_e = expd[:, 2 * GROUP_W:3 * GROUP_W]
        g_ssd = expd[:, 3 * GROUP_W:4 * GROUP_W]

        hg = p_ref[cur, HG_OFF:HG_OFF + 3 * GROUP_W]
        q_a = _silu(hg[:, 0:GROUP_W]) * (HEAD_DIM ** -0.5)
        f_a = lb + (1.0 - lb) * jax.nn.sigmoid(hg[:, GROUP_W:2 * GROUP_W])
        k_a = 1.0 - f_a
        v_a = hg[:, 2 * GROUP_W:3 * GROUP_W]

        xw = p_ref[win, GDN_OFF:GDN_OFF + 3 * GROUP_W]
        xc = gconv_ref[0:1, :] * xw[TAIL - 3:TAIL - 3 + CHUNK, :]
        for j in range(1, CONV_K):
            xc = xc + gconv_ref[j:j + 1, :] * xw[TAIL - 3 + j:TAIL - 3 + j + CHUNK, :]
        xc = _silu(xc)
        q_b = xc[:, 0:GROUP_W]
        k_b = xc[:, GROUP_W:2 * GROUP_W]
        v_b = xc[:, 2 * GROUP_W:3 * GROUP_W]

        xw = p_ref[win, SSM_OFF:SSM_OFF + 3 * GROUP_W]
        xc = sconvw_ref[0:1, :] * xw[TAIL - 3:TAIL - 3 + CHUNK, :]
        for j in range(1, CONV_K):
            xc = xc + sconvw_ref[j:j + 1, :] * xw[TAIL - 3 + j:TAIL - 3 + j + CHUNK, :]
        xc = _silu(xc + sconvb_ref[...])
        x_c = xc[:, 0:GROUP_W]
        b_c = xc[:, GROUP_W:2 * GROUP_W]
        c_c = xc[:, 2 * GROUP_W:3 * GROUP_W]

        xw = p_ref[win, RWKV_OFF:RWKV_OFF + RWKV_W]
        xcur = xw[TAIL:TAIL + CHUNK, :]
        xr = xcur + (xw[TAIL - 1:TAIL - 1 + CHUNK, :] - xcur) * mu_ref[...]
        r_d = xr[:, 0:GROUP_W]
        k_d = xr[:, GROUP_W:2 * GROUP_W]
        v_d = xr[:, 2 * GROUP_W:3 * GROUP_W]
        wd = xr[:, 3 * GROUP_W:3 * GROUP_W + LOW_RANK]
        ad = xr[:, 3 * GROUP_W + LOW_RANK:3 * GROUP_W + 2 * LOW_RANK]
        w_d = -jax.nn.softplus(-(vec(V_W0) + _mm(jnp.tanh(wd), wup_ref[...]))) - 0.5
        lw = -jnp.exp(w_d)
        a_d = jax.nn.sigmoid(vec(V_A0) + _mm(ad, aup_ref[...]))
        kk = k_d * vec(V_KK)
        k_d = k_d * (1.0 + (a_d - 1.0) * vec(V_KA))

        sq = head_sum(jnp.concatenate([q_b * q_b, k_b * k_b, kk * kk], axis=0))
        q_b = q_b * lax.rsqrt(sq[0:CHUNK] + L2_EPS) * (HEAD_DIM ** -0.5)
        k_b = k_b * lax.rsqrt(sq[CHUNK:2 * CHUNK] + L2_EPS)
        kk = kk * lax.rsqrt(sq[2 * CHUNK:3 * CHUNK] + L2_EPS)

        cums = _mm_const_l(ltri_ref[...], jnp.concatenate(
            [jnp.log(f_a), lw, g_gdn, g_ssd, g_gdn * strict, g_ssd * strict], axis=1))
        cum_a = cums[:, 0:GROUP_W]
        cum_d = cums[:, GROUP_W:2 * GROUP_W]
        cum_b = cums[:, 2 * GROUP_W:3 * GROUP_W]
        cum_c = cums[:, 3 * GROUP_W:4 * GROUP_W]
        gam_b = causal * jnp.exp(jnp.minimum(cums[:, 4 * GROUP_W:5 * GROUP_W], 0.0))
        gam_c = causal * jnp.exp(jnp.minimum(cums[:, 5 * GROUP_W:6 * GROUP_W], 0.0))

        kb = k_b * beta_e
        kq = _mm_nt(jnp.concatenate([kb, q_b], axis=0), bd(k_b))
        n_b = strict * kq[0:CHUNK] * gam_b
        qk_b = causal * kq[CHUNK:2 * CHUNK] * gam_b

        p_inc = jnp.exp(cum_d)
        p_inv = jnp.exp(-cum_d)
        p_last = p_inc[CHUNK - 1:CHUNK, :]
        a_t = -kk * jnp.exp(cum_d - lw)
        b_t = kk * a_d * p_inv
        k_t = k_d * p_inv
        r_t = r_d * p_inc
        ar = jnp.concatenate([a_t, r_t], axis=0)
        arb = _mm_nt(ar, bd(b_t))
        ark = _mm_nt(ar, bd(k_t))
        ab = strict * arb[0:CHUNK]
        rb = causal * arb[CHUNK:2 * CHUNK]
        ak = strict * ark[0:CHUNK]
        rk = causal * ark[CHUNK:2 * CHUNK]

        cum_last = cum_a[CHUNK - 1:CHUNK, :]
        q_in = q_a * jnp.exp(cum_a)
        k_out = k_a * jnp.exp(cum_last - cum_a)
        o_diag = []
        a_rows = [jnp.zeros((SUB, GROUP_W), F32)]
        for i in range(CHUNK // SUB):
            lo = i * SUB
            cum_i = cum_a[lo:lo + SUB, :]
            q_i = q_a[lo:lo + SUB, :]
            zs = []
            for sl in range(SUB):
                s = lo + sl
                t0 = 0 if sl < HALF else HALF
                e = jnp.exp(jnp.minimum(cum_i[t0:SUB, :] - cum_a[s:s + 1, :], 0.0))
                zs.append(q_i[t0:SUB, :] * e * k_a[s:s + 1, :])
            coef = _mm(jnp.concatenate(zs, axis=0), bdm_ref[...])
            o_top = jnp.zeros((HALF, GROUP_W), F32)
            o_bot = jnp.zeros((HALF, GROUP_W), F32)
            row = 0
            for sl in range(SUB):
                v_s = v_a[lo + sl:lo + sl + 1, :]
                if sl < HALF:
                    o_top = o_top + jnp.where(half_r >= sl, coef[row:row + HALF, :], 0.0) * v_s
                    o_bot = o_bot + coef[row + HALF:row + SUB, :] * v_s
                    row += SUB
                else:
                    o_bot = o_bot + jnp.where(half_r >= sl - HALF, coef[row:row + HALF, :], 0.0) * v_s
                    row += HALF
            o_diag += [o_top, o_bot]
            if i > 0:
                cs = cum_a[lo - 1:lo, :]
                qd = q_i * jnp.exp(cum_i - cs)
                kh = jnp.where(row_c < lo, k_a * jnp.exp(jnp.minimum(cs - cum_a, 0.0)), 0.0)
                a_rows.append(_mm_nt(qd, bd(kh)))
        st_a = s_hg[...]
        o_a = (jnp.concatenate(o_diag, axis=0) + _mm(jnp.concatenate(a_rows, axis=0), bd(v_a))
               + _mm_nt(q_in, st_a))
        s_hg[...] = st_a * jnp.exp(cum_last) + bdmf_ref[...] * _mm_tn(v_a, k_out)

        cb = _mm_nt(c_c, bd(b_c, GROUP_W // SSM_GROUPS))
        xv = x_c * dt_e
        c_last = cum_c[CHUNK - 1:CHUNK, :]
        st_c = s_ssm[...]
        o_c = _mm(cb * gam_c, bd(xv)) + jnp.exp(cum_c) * _mm(c_c, st_c)
        s_ssm[...] = st_c * jnp.exp(c_last) + grpmf_ref[...] * _mm_tn(b_c, xv * jnp.exp(c_last - cum_c))

        rows = pl.ds(r0, CHUNK)
        y_a = o_a * lax.rsqrt(head_sum(o_a * o_a) * (1.0 / HEAD_DIM) + NORM_EPS) * vec(V_HG_G)
        y_ref[rows, 0:GROUP_W] = (y_a * _silu(gates[:, 0:GROUP_W])).astype(y_ref.dtype)
        yv = (o_c + x_c * vec(V_SSM_D)) * _silu(gates[:, 2 * GROUP_W:3 * GROUP_W])
        gw = GROUP_W // SSM_GROUPS
        for grp in range(SSM_GROUPS):
            yg = yv[:, grp * gw:(grp + 1) * gw]
            yg = yg * lax.rsqrt(jnp.mean(yg * yg, axis=-1, keepdims=True) + NORM_EPS)
            yg = yg * vec_ref[V_SSM_G:V_SSM_G + 1, grp * gw:(grp + 1) * gw]
            y_ref[rows, 2 * GROUP_W + grp * gw:2 * GROUP_W + (grp + 1) * gw] = yg.astype(y_ref.dtype)
        return dict(rows=rows, n_b=n_b, ab=ab, beta_e=beta_e, q_b=q_b, k_b=k_b, v_b=v_b, kb=kb, cum_b=cum_b,
                    qk_b=qk_b, ar=ar, b_t=b_t, k_t=k_t, p_last=p_last, ak=ak, rb=rb, rk=rk, v_d=v_d, r_d=r_d,
                    k_d=k_d, gate_b=gates[:, GROUP_W:2 * GROUP_W], gate_d=gates[:, 3 * GROUP_W:4 * GROUP_W])

    def phase_b(a, x_b, x_d):
        (rows, beta_e, q_b, k_b, v_b, kb, cum_b, qk_b, ar, b_t, k_t, p_last, ak, rb, rk, v_d, r_d, k_d) = (
            a[n] for n in ("rows", "beta_e", "q_b", "k_b", "v_b", "kb", "cum_b", "qk_b", "ar", "b_t", "k_t",
                           "p_last", "ak", "rb", "rk", "v_d", "r_d", "k_d"))
        ecum_b = jnp.exp(cum_b)
        b_last = cum_b[CHUNK - 1:CHUNK, :]
        st_b = s_gdn[...]
        st_d = s_rwkv[...]
        uu = mm_bd(x_b, v_b * beta_e)
        ww = mm_bd(x_b, kb * ecum_b)
        rhs_d = _mm_nt(ar, st_d)
        wq = _mm(jnp.concatenate([ww, q_b * ecum_b], axis=0), st_b)
        v_new = uu - wq[0:CHUNK]
        bd_vd = bd(v_d)
        u_d = mm_bd(x_d, rhs_d[0:CHUNK] + _mm(ak, bd_vd))
        o_b = wq[CHUNK:2 * CHUNK] + _mm(qk_b, bd(v_new))
        o_d = rhs_d[CHUNK:2 * CHUNK] + _mm(rb, bd(u_d)) + _mm(rk, bd_vd)
        bdm = bdmf_ref[...]
        s_gdn[...] = st_b * jnp.exp(b_last) + bdm * _mm_tn(k_b * jnp.exp(b_last - cum_b), v_new)
        s_rwkv[...] = st_d * p_last + bdm * (_mm_tn(u_d, b_t * p_last) + _mm_tn(v_d, k_t * p_last))

        sums = head_sum(jnp.concatenate([o_b * o_b, o_d, r_d * k_d * vec(V_RK)], axis=0))
        y_b = o_b * lax.rsqrt(sums[0:CHUNK] * (1.0 / HEAD_DIM) + NORM_EPS) * vec(V_GDN_G)
        oc = o_d - sums[CHUNK:2 * CHUNK] * (1.0 / HEAD_DIM)
        var = head_sum(oc * oc) * (1.0 / HEAD_DIM)
        y_d = oc * lax.rsqrt(var + RWKV_GN_EPS) * vec(V_LNW) + vec(V_LNB) + sums[2 * CHUNK:3 * CHUNK] * v_d
        y_ref[rows, GROUP_W:2 * GROUP_W] = (y_b * _silu(a["gate_b"])).astype(y_ref.dtype)
        y_ref[rows, 3 * GROUP_W:4 * GROUP_W] = (y_d * _silu(a["gate_d"])).astype(y_ref.dtype)

    def chunks_body(i, carry):
        slots_used.clear()
        parts = [phase_a(i * LOCKSTEP + j) for j in range(LOCKSTEP)]
        inv = tri_inverses([m for a in parts for m in (a["n_b"], -a["ab"])])
        for j, a in enumerate(parts):
            phase_b(a, inv[2 * j], inv[2 * j + 1])
        return carry

    assert n_chunks % LOCKSTEP == 0
    lax.fori_loop(0, n_chunks // LOCKSTEP, chunks_body, 0)

    out = _dot(y_ref[...], wout_ref[...])
    out = out * lax.rsqrt(jnp.mean(out * out, axis=-1, keepdims=True) + NORM_EPS) * gpost_ref[...]
    o_ref[...] = x_ref[...] + out


def _layer_call(layer, h, *params):
    b, l, d = h.shape
    tl = min(TIME_TILE, l)
    assert l % tl == 0 and tl % CHUNK == 0 and d == D_MODEL
    n_chunks = tl // CHUNK
    bd_state = pltpu.VMEM((N_HEADS * HEAD_DIM, N_HEADS * HEAD_DIM), F32)

    def full(a):
        return pl.BlockSpec(a.shape, lambda i, j: (0,) * a.ndim)

    return pl.pallas_call(
        functools.partial(_layer_kernel, layer, n_chunks),
        grid=(b, l // tl),
        in_specs=[pl.BlockSpec((None, tl, d), lambda i, j: (i, j, 0))] + [full(a) for a in params],
        out_specs=pl.BlockSpec((None, tl, d), lambda i, j: (i, j, 0)),
        out_shape=jax.ShapeDtypeStruct(h.shape, h.dtype),
        scratch_shapes=[
            pltpu.VMEM((tl + TAIL, PROJ_COLS), F32),
            pltpu.VMEM((tl, D_MODEL), BF16),
            pltpu.VMEM((N_BD_SLOTS, N_HEADS * CHUNK, GROUP_W), BF16),
            bd_state,
            bd_state,
            bd_state,
            bd_state,
        ],
        compiler_params=pltpu.CompilerParams(
            dimension_semantics=("arbitrary", "arbitrary"),
            vmem_limit_bytes=V7X_VMEM_LIMIT_BYTES),
        name=f"hybrid_layer{layer}",
    )(h, *params)


W_IN_PIECES = (
    ((0, 1536),),
    ((1544, 2312),),
    ((2316, 3212),),
    ((3212, 4236),),
    ((1536, 1544), (2312, 2316), (2312, 2316)),
)


def _repack_kernel(w_ref, *out_refs):
    w = w_ref[...]
    for pieces, o_ref in zip(W_IN_PIECES, out_refs):
        cols = [w[:, a:b] for a, b in pieces]
        used = sum(b - a for a, b in pieces)
        if used < o_ref.shape[1]:
            cols.append(jnp.zeros((w.shape[0], o_ref.shape[1] - used), w.dtype))
        o_ref[...] = (cols[0] if len(cols) == 1 else jnp.concatenate(cols, axis=1)).astype(o_ref.dtype)


def _split_w_in(w_in, layer):
    _, d, cols = w_in.shape
    widths = [-(-sum(b - a for a, b in p) // 128) * 128 for p in W_IN_PIECES]
    assert sum(widths) == PROJ_COLS and d % CAST_ROWS == 0
    return pl.pallas_call(
        _repack_kernel, grid=(d // CAST_ROWS,),
        in_specs=[pl.BlockSpec((None, CAST_ROWS, cols), lambda i: (layer, i, 0))],
        out_specs=[pl.BlockSpec((CAST_ROWS, w), lambda i: (i, 0)) for w in widths],
        out_shape=[jax.ShapeDtypeStruct((d, w), BF16) for w in widths], name="repack_w_in",
    )(w_in)


def _constants():
    ri = lax.broadcasted_iota(jnp.int32, (CHUNK, CHUNK), 0)
    ci = lax.broadcasted_iota(jnp.int32, (CHUNK, CHUNK), 1)
    ltri = (ri >= ci).astype(BF16)
    r2 = lax.broadcasted_iota(jnp.int32, (GROUP_W, GROUP_W), 0)
    c2 = lax.broadcasted_iota(jnp.int32, (GROUP_W, GROUP_W), 1)
    bdm = ((r2 // HEAD_DIM) == (c2 // HEAD_DIM)).astype(BF16)
    gw = GROUP_W // SSM_GROUPS
    grpm = ((r2 // gw) == (c2 // gw)).astype(F32)
    er = lax.broadcasted_iota(jnp.int32, (SMALL_W, 4 * GROUP_W), 0)
    ec = lax.broadcasted_iota(jnp.int32, (SMALL_W, 4 * GROUP_W), 1)
    eexp = (er == ec // HEAD_DIM).astype(BF16)
    levels = []
    for lev in range(N_LEVELS):
        s = 2 ** lev
        levels.append(((ri // (2 * s)) == (ci // (2 * s))) & (((ri // s) % 2) == 1) & (((ci // s) % 2) == 0))
    masks = jnp.stack([jnp.tile(m.astype(F32), (1, N_HEADS)) for m in [ri >= ci, ri > ci, ri == ci] + levels])
    return ltri, bdm, bdm.astype(F32), grpm, eexp, masks


def kernel(x, pre_norm_g, w_in, hgrn_lower_bounds, hgrn_norm_g, gdn_conv_w, gdn_a_log, gdn_dt_bias, gdn_norm_g,
           ssm_conv_w, ssm_conv_b, ssm_a_log, ssm_dt_bias, ssm_d, ssm_norm_g, rwkv_mu, rwkv_w0, rwkv_w_up,
           rwkv_a0, rwkv_a_up, rwkv_k_k, rwkv_k_a, rwkv_r_k, rwkv_ln_w, rwkv_ln_b, w_out, post_norm_g):
    depth = w_in.shape[0]
    consts = _constants()
    zeros4 = jnp.zeros((N_HEADS,), F32)
    ones4 = jnp.ones((N_HEADS,), F32)
    pad = jnp.zeros((SMALL_W - 4 * N_HEADS,), F32)
    h = x
    for l in range(depth):
        vecs = jnp.stack([
            jnp.tile(hgrn_norm_g[l], N_HEADS), jnp.tile(gdn_norm_g[l], N_HEADS),
            jnp.repeat(ssm_d[l], HEAD_DIM), ssm_norm_g[l], rwkv_w0[l], rwkv_a0[l], rwkv_k_k[l],
            rwkv_k_a[l], rwkv_r_k[l], rwkv_ln_w[l], rwkv_ln_b[l]] + [hgrn_lower_bounds[i] for i in range(depth)])
        small = jnp.stack([
            jnp.concatenate([zeros4, gdn_dt_bias[l], ssm_dt_bias[l], ssm_dt_bias[l], pad]),
            jnp.concatenate([zeros4, gdn_a_log[l], zeros4, ssm_a_log[l], pad]),
            jnp.concatenate([zeros4, zeros4, ones4, zeros4, pad])])
        h = _layer_call(
            l, h, pre_norm_g[l][None, :], *_split_w_in(w_in, l), w_out[l].astype(BF16), post_norm_g[l][None, :],
            vecs.astype(F32), small.astype(F32), gdn_conv_w[l], ssm_conv_w[l], ssm_conv_b[l][None, :],
            rwkv_mu[l][None, :], rwkv_w_up[l].astype(BF16), rwkv_a_up[l].astype(BF16), *consts)
    return h
```

```python
import functools

import jax
import jax.numpy as jnp
from jax import lax
from jax.experimental import pallas as pl
from jax.experimental.pallas import tpu as pltpu

F32 = jnp.float32
BF16 = jnp.bfloat16

D_MODEL = 1024
GROUP_W = 256
HEAD_DIM = 64
N_HEADS = 4
CHUNK = 64
SUBLANES = 8
LANES = 128
CONV_K = 4
SSM_STATE = 128
SSM_GROUPS = 2
LOW_RANK = 64
NORM_EPS = 1e-6
L2_EPS = 1e-6
RWKV_GN_EPS = 64e-5

HG_OFF = 0
GDN_OFF = 768
SSM_OFF = 1536
RWKV_OFF = 2304
RWKV_W = 3 * GROUP_W + 2 * LOW_RANK
GATE_OFF = 3200
SMALL_OFF = 4224
SMALL_W = 128
PROJ_COLS = 4352

TAIL = 8
TIME_TILE = 256
CAST_ROWS = 256
V7X_VMEM_LIMIT_BYTES = 52 * 1024 * 1024

(V_HG_G, V_GDN_G, V_SSM_D, V_SSM_G, V_W0, V_A0, V_KK, V_KA, V_RK, V_LNW, V_LNB) = range(11)
V_LB = 11

M_CAUSAL, M_STRICT, M_EYE, M_LEVEL0 = 0, 1, 2, 3
N_LEVELS = 6
INVERSE_SPLIT = 3


def _sigmoid(x):
    return 0.5 * jnp.tanh(0.5 * x) + 0.5


def _silu(x):
    h = 0.5 * x
    return h + h * jnp.tanh(h)


def _split(a):
    hi = a.astype(BF16)
    lo = (a - hi.astype(F32)).astype(BF16)
    return hi, lo


def _dot(a, b):
    return jnp.dot(a, b, preferred_element_type=F32)


def _mm(a, b):
    return _dot(a.astype(BF16), b.astype(BF16))


def _mm_nt(a, b):
    return lax.dot_general(a.astype(BF16), b.astype(BF16), (((1,), (1,)), ((), ())),
                           preferred_element_type=F32)


def _mm_tn(a, b):
    return lax.dot_general(a.astype(BF16), b.astype(BF16), (((0,), (0,)), ((), ())),
                           preferred_element_type=F32)


def _mm_const_l(c, b):
    hi, lo = _split(b)
    return _dot(c, hi) + _dot(c, lo)


def _mm_const_r(a, c):
    hi, lo = _split(a)
    n = a.shape[0]
    t = _dot(jnp.concatenate([hi, lo], axis=0), c)
    return t[0:n] + t[n:2 * n]


def _layer_kernel(layer, n_chunks,
                  x_ref, gpre_ref, win0_ref, win1_ref, win2_ref, win3_ref, win4_ref,
                  wout_ref, gpost_ref, vec_ref, small_ref,
                  gconv_ref, sconvw_ref, sconvb_ref, mu_ref, wup_ref, aup_ref,
                  ltri_ref, bdm_ref, bdmf_ref, grpmf_ref, eexp_ref, masks_ref,
                  o_ref,
                  p_ref, y_ref, s_hg, s_gdn, s_ssm, s_rwkv):
    t_idx = pl.program_id(1)
    tl = n_chunks * CHUNK
    win_refs = (win0_ref, win1_ref, win2_ref, win3_ref, win4_ref)
    assert sum(w.shape[1] for w in win_refs) == PROJ_COLS

    @pl.when(t_idx == 0)
    def _():
        p_ref[0:TAIL, :] = jnp.zeros((TAIL, PROJ_COLS), F32)
        s_hg[...] = jnp.zeros_like(s_hg)
        s_gdn[...] = jnp.zeros_like(s_gdn)
        s_ssm[...] = jnp.zeros_like(s_ssm)
        s_rwkv[...] = jnp.zeros_like(s_rwkv)

    @pl.when(t_idx != 0)
    def _():
        p_ref[0:TAIL, :] = p_ref[tl:tl + TAIL, :]

    h = x_ref[...]
    u = h * lax.rsqrt(jnp.mean(h * h, axis=-1, keepdims=True) + NORM_EPS) * gpre_ref[...]
    ub = u.astype(BF16)
    col = 0
    for w_ref in win_refs:
        p_ref[TAIL:TAIL + tl, col:col + w_ref.shape[1]] = _dot(ub, w_ref[...])
        col += w_ref.shape[1]

    sub_r = lax.broadcasted_iota(jnp.int32, (SUBLANES, GROUP_W), 0)
    lane_s = lax.broadcasted_iota(jnp.int32, (1, SMALL_W), 1)

    def vec(i):
        return vec_ref[i:i + 1, :]

    def mask(i):
        return masks_ref[i]

    lane_t = lax.broadcasted_iota(jnp.int32, (1, LANES), 1)
    zero_tile = jnp.zeros((CHUNK, LANES), BF16)

    def bd(y, width=HEAD_DIM):
        yb = y.astype(BF16)
        rows = []
        for hd in range(N_HEADS):
            lo = (hd * HEAD_DIM // width) * width
            tiles = []
            for t0 in range(0, GROUP_W, LANES):
                first, last = max(lo, t0), min(lo + width, t0 + LANES)
                if first >= last:
                    tiles.append(zero_tile)
                elif last - first == LANES:
                    tiles.append(yb[:, t0:t0 + LANES])
                else:
                    keep = ((lane_t >= first - t0) & (lane_t < last - t0)).astype(BF16)
                    tiles.append(yb[:, t0:t0 + LANES] * keep)
            rows.append(jnp.concatenate(tiles, axis=1))
        return jnp.concatenate(rows, axis=0)

    def head_sum(v):
        return _mm(v, bdm_ref[...])

    def mm_bd(l, r):
        return _mm(l, bd(r))

    def tri_inverse_levels(a, first, last):
        ns = a["systems"]
        for lev in range(first, last):
            m = mask(M_LEVEL0 + lev)
            if lev == 0:
                a["inverses"] = [mask(M_EYE) - m * n for n in ns]
                continue
            ws = [mm_bd(m * n, x) for n, x in zip(ns, a["inverses"])]
            yield
            a["inverses"] = [x - mm_bd(x, w) for x, w in zip(a["inverses"], ws)]
            yield

    depth = vec_ref.shape[0] - V_LB
    lbs = [vec(V_LB + l) for l in range(depth)]
    lb_max = functools.reduce(jnp.maximum, lbs)
    lb_exp = [jnp.exp(v - lb_max) for v in lbs]
    lb_den = functools.reduce(lambda a, b: a + b, lb_exp)
    sm = [e / lb_den for e in lb_exp]
    lb = functools.reduce(lambda a, b: a + b, sm[:layer + 1]) - sm[0]

    small_bias = small_ref[0:1, :]
    plain = small_ref[2:3, :]
    small_mult = plain - (1.0 - plain) * jnp.exp(small_ref[1:2, :])

    def phase_a(c, a):
        r0 = c * CHUNK
        cur = pl.ds(r0 + TAIL, CHUNK)

        def causal_conv(off, w_ref):
            y = None
            for j in range(CONV_K):
                x_j = p_ref[pl.ds(r0 + TAIL - (CONV_K - 1) + j, CHUNK), off:off + 3 * GROUP_W]
                y = w_ref[j:j + 1, :] * x_j if y is None else y + w_ref[j:j + 1, :] * x_j
            return y
        causal = mask(M_CAUSAL)
        strict = mask(M_STRICT)

        gates = p_ref[cur, GATE_OFF:GATE_OFF + 4 * GROUP_W]

        small = p_ref[cur, SMALL_OFF:SMALL_OFF + SMALL_W]
        small_act = jnp.where(lane_s < N_HEADS, _sigmoid(small),
                              jax.nn.softplus(small + small_bias) * small_mult)
        expd = _mm_const_r(small_act, eexp_ref[...])
        beta_e = expd[:, 0:GROUP_W]
        g_gdn = expd[:, GROUP_W:2 * GROUP_W]
        dt_e = expd[:, 2 * GROUP_W:3 * GROUP_W]
        g_ssd = expd[:, 3 * GROUP_W:4 * GROUP_W]
        yield

        hg = p_ref[cur, HG_OFF:HG_OFF + 3 * GROUP_W]
        q_a = _silu(hg[:, 0:GROUP_W]) * (HEAD_DIM ** -0.5)
        f_a = lb + (1.0 - lb) * _sigmoid(hg[:, GROUP_W:2 * GROUP_W])
        k_a = 1.0 - f_a
        v_a = hg[:, 2 * GROUP_W:3 * GROUP_W]
        yield

        xc = _silu(causal_conv(GDN_OFF, gconv_ref))
        q_b = xc[:, 0:GROUP_W]
        k_b = xc[:, GROUP_W:2 * GROUP_W]
        v_b = xc[:, 2 * GROUP_W:3 * GROUP_W]
        yield

        xc = _silu(causal_conv(SSM_OFF, sconvw_ref) + sconvb_ref[...])
        x_c = xc[:, 0:GROUP_W]
        b_c = xc[:, GROUP_W:2 * GROUP_W]
        c_c = xc[:, 2 * GROUP_W:3 * GROUP_W]
        yield

        xcur = p_ref[cur, RWKV_OFF:RWKV_OFF + RWKV_W]
        xr = xcur + (p_ref[pl.ds(r0 + TAIL - 1, CHUNK), RWKV_OFF:RWKV_OFF + RWKV_W] - xcur) * mu_ref[...]
        r_d = xr[:, 0:GROUP_W]
        k_d = xr[:, GROUP_W:2 * GROUP_W]
        v_d = xr[:, 2 * GROUP_W:3 * GROUP_W]
        wd = xr[:, 3 * GROUP_W:3 * GROUP_W + LOW_RANK]
        ad = xr[:, 3 * GROUP_W + LOW_RANK:3 * GROUP_W + 2 * LOW_RANK]
        w_d = -jax.nn.softplus(-(vec(V_W0) + _mm(jnp.tanh(wd), wup_ref[...]))) - 0.5
        lw = -jnp.exp(w_d)
        a_d = _sigmoid(vec(V_A0) + _mm(ad, aup_ref[...]))
        kk = k_d * vec(V_KK)
        k_d = k_d * (1.0 + (a_d - 1.0) * vec(V_KA))
        yield

        sq = head_sum(jnp.concatenate([q_b * q_b, k_b * k_b, kk * kk, q_a * k_a], axis=0))
        q_b = q_b * lax.rsqrt(sq[0:CHUNK] + L2_EPS) * (HEAD_DIM ** -0.5)
        k_b = k_b * lax.rsqrt(sq[CHUNK:2 * CHUNK] + L2_EPS)
        kk = kk * lax.rsqrt(sq[2 * CHUNK:3 * CHUNK] + L2_EPS)
        yield

        cums = _mm_const_l(ltri_ref[...], jnp.concatenate(
            [jnp.log(f_a), lw, g_gdn, g_ssd, g_gdn * strict, g_ssd * strict], axis=1))
        cum_a = cums[:, 0:GROUP_W]
        cum_d = cums[:, GROUP_W:2 * GROUP_W]
        cum_b = cums[:, 2 * GROUP_W:3 * GROUP_W]
        cum_c = cums[:, 3 * GROUP_W:4 * GROUP_W]
        gam_b = causal * jnp.exp(jnp.minimum(cums[:, 4 * GROUP_W:5 * GROUP_W], 0.0))
        gam_c = causal * jnp.exp(jnp.minimum(cums[:, 5 * GROUP_W:6 * GROUP_W], 0.0))
        yield

        kb = k_b * beta_e
        kq = _mm_nt(jnp.concatenate([kb, q_b], axis=0), bd(k_b))
        n_b = strict * kq[0:CHUNK] * gam_b
        qk_b = causal * kq[CHUNK:2 * CHUNK] * gam_b
        yield

        p_inc = jnp.exp(cum_d)
        p_inv = jnp.exp(-cum_d)
        p_last = p_inc[CHUNK - 1:CHUNK, :]
        a_t = -kk * jnp.exp(cum_d - lw)
        b_t = kk * a_d * p_inv
        k_t = k_d * p_inv
        r_t = r_d * p_inc
        ar = jnp.concatenate([a_t, r_t], axis=0)
        arb = _mm_nt(ar, bd(b_t))
        ark = _mm_nt(ar, bd(k_t))
        ab = strict * arb[0:CHUNK]
        rb = causal * arb[CHUNK:2 * CHUNK]
        ak = strict * ark[0:CHUNK]
        rk = causal * ark[CHUNK:2 * CHUNK]
        a.update(rows=pl.ds(r0, CHUNK), systems=[n_b, -ab], beta_e=beta_e, q_b=q_b, k_b=k_b, v_b=v_b, kb=kb,
                 cum_b=cum_b, qk_b=qk_b, ar=ar, b_t=b_t, k_t=k_t, p_last=p_last, ak=ak, rb=rb, rk=rk, v_d=v_d,
                 r_d=r_d, k_d=k_d, gate_b=gates[:, GROUP_W:2 * GROUP_W], gate_d=gates[:, 3 * GROUP_W:4 * GROUP_W])
        yield

        cum_last = cum_a[CHUNK - 1:CHUNK, :]
        q_in = q_a * jnp.exp(cum_a)
        k_out = k_a * jnp.exp(cum_last - cum_a)
        a_w = mask(M_EYE) * sq[3 * CHUNK:4 * CHUNK] + mask(M_LEVEL0) * _mm_nt(q_a * f_a, bd(k_a))
        yield
        for lev in range(1, N_LEVELS):
            b = 2 ** lev
            groups = []
            for g in range(CHUNK // SUBLANES):
                first = g * SUBLANES
                if 2 * b >= SUBLANES:
                    r = first // (2 * b) * (2 * b) + b - 1
                    groups.append(jnp.broadcast_to(cum_a[r:r + 1, :], (SUBLANES, GROUP_W)))
                else:
                    groups.append(jnp.where(
                        sub_r < 2 * b,
                        jnp.broadcast_to(cum_a[first + b - 1:first + b, :], (SUBLANES, GROUP_W)),
                        jnp.broadcast_to(cum_a[first + 3 * b - 1:first + 3 * b, :], (SUBLANES, GROUP_W))))
            d = cum_a - jnp.concatenate(groups, axis=0)
            a_w = a_w + mask(M_LEVEL0 + lev) * _mm_nt(q_a * jnp.exp(jnp.minimum(d, 0.0)),
                                                     bd(k_a * jnp.exp(jnp.minimum(-d, 0.0))))
            yield
        st_a = s_hg[...]
        o_a = _mm(a_w, bd(v_a)) + _mm_nt(q_in, st_a)
        s_hg[...] = st_a * jnp.exp(cum_last) + bdmf_ref[...] * _mm_tn(v_a, k_out)
        yield

        cb = _mm_nt(c_c, bd(b_c, GROUP_W // SSM_GROUPS))
        xv = x_c * dt_e
        c_last = cum_c[CHUNK - 1:CHUNK, :]
        st_c = s_ssm[...]
        o_c = _mm(cb * gam_c, bd(xv)) + jnp.exp(cum_c) * _mm(c_c, st_c)
        s_ssm[...] = st_c * jnp.exp(c_last) + grpmf_ref[...] * _mm_tn(b_c, xv * jnp.exp(c_last - cum_c))
        yield

        rows = a["rows"]
        y_a = o_a * lax.rsqrt(head_sum(o_a * o_a) * (1.0 / HEAD_DIM) + NORM_EPS) * vec(V_HG_G)
        y_ref[rows, 0:GROUP_W] = (y_a * _silu(gates[:, 0:GROUP_W])).astype(y_ref.dtype)
        yv = (o_c + x_c * vec(V_SSM_D)) * _silu(gates[:, 2 * GROUP_W:3 * GROUP_W])
        gw = GROUP_W // SSM_GROUPS
        for grp in range(SSM_GROUPS):
            yg = yv[:, grp * gw:(grp + 1) * gw]
            yg = yg * lax.rsqrt(jnp.mean(yg * yg, axis=-1, keepdims=True) + NORM_EPS)
            yg = yg * vec_ref[V_SSM_G:V_SSM_G + 1, grp * gw:(grp + 1) * gw]
            y_ref[rows, 2 * GROUP_W + grp * gw:2 * GROUP_W + (grp + 1) * gw] = yg.astype(y_ref.dtype)

    def phase_b(a):
        x_b, x_d = a["inverses"]
        (rows, beta_e, q_b, k_b, v_b, kb, cum_b, qk_b, ar, b_t, k_t, p_last, ak, rb, rk, v_d, r_d, k_d) = (
            a[n] for n in ("rows", "beta_e", "q_b", "k_b", "v_b", "kb", "cum_b", "qk_b", "ar", "b_t", "k_t",
                           "p_last", "ak", "rb", "rk", "v_d", "r_d", "k_d"))
        ecum_b = jnp.exp(cum_b)
        b_last = cum_b[CHUNK - 1:CHUNK, :]
        st_b = s_gdn[...]
        st_d = s_rwkv[...]
        uu = mm_bd(x_b, v_b * beta_e)
        ww = mm_bd(x_b, kb * ecum_b)
        rhs_d = _mm_nt(ar, st_d)
        yield
        wq = _mm(jnp.concatenate([ww, q_b * ecum_b], axis=0), st_b)
        v_new = uu - wq[0:CHUNK]
        bd_vd = bd(v_d)
        rhs_u = rhs_d[0:CHUNK] + _mm(ak, bd_vd)
        yield
        u_d = mm_bd(x_d, rhs_u)
        yield
        o_b = wq[CHUNK:2 * CHUNK] + _mm(qk_b, bd(v_new))
        o_d = rhs_d[CHUNK:2 * CHUNK] + _mm(rb, bd(u_d)) + _mm(rk, bd_vd)
        yield
        bdm = bdmf_ref[...]
        s_gdn[...] = st_b * jnp.exp(b_last) + bdm * _mm_tn(k_b * jnp.exp(b_last - cum_b), v_new)
        s_rwkv[...] = st_d * p_last + bdm * (_mm_tn(u_d, b_t * p_last) + _mm_tn(v_d, k_t * p_last))
        yield

        sums = head_sum(jnp.concatenate([o_b * o_b, o_d, r_d * k_d * vec(V_RK)], axis=0))
        y_b = o_b * lax.rsqrt(sums[0:CHUNK] * (1.0 / HEAD_DIM) + NORM_EPS) * vec(V_GDN_G)
        oc = o_d - sums[CHUNK:2 * CHUNK] * (1.0 / HEAD_DIM)
        var = head_sum(oc * oc) * (1.0 / HEAD_DIM)
        y_d = oc * lax.rsqrt(var + RWKV_GN_EPS) * vec(V_LNW) + vec(V_LNB) + sums[2 * CHUNK:3 * CHUNK] * v_d
        y_ref[rows, GROUP_W:2 * GROUP_W] = (y_b * _silu(a["gate_b"])).astype(y_ref.dtype)
        y_ref[rows, 3 * GROUP_W:4 * GROUP_W] = (y_d * _silu(a["gate_d"])).astype(y_ref.dtype)

    parts = [{} for _ in range(n_chunks)]
    threads = (
        (0, 1, lambda k: phase_a(k, parts[k])),
        (1, 4, lambda k: tri_inverse_levels(parts[k], 0, INVERSE_SPLIT)),
        (2, 3, lambda k: tri_inverse_levels(parts[k], INVERSE_SPLIT, N_LEVELS)),
        (3, 3, lambda k: phase_b(parts[k])),
    )
    for slot in range(n_chunks + len(threads) - 1):
        live = [(period, make(slot - lag)) for lag, period, make in threads if 0 <= slot - lag < n_chunks]
        tick = 0
        while live:
            for entry in list(live):
                if tick % entry[0] == 0 and next(entry[1], "done") == "done":
                    live.remove(entry)
            tick += 1

    out = _dot(y_ref[...], wout_ref[...])
    out = out * lax.rsqrt(jnp.mean(out * out, axis=-1, keepdims=True) + NORM_EPS) * gpost_ref[...]
    o_ref[...] = x_ref[...] + out


def _layer_call(layer, h, *params):
    b, l, d = h.shape
    tl = min(TIME_TILE, l)
    assert l % tl == 0 and tl % CHUNK == 0 and d == D_MODEL
    n_chunks = tl // CHUNK
    bd_state = pltpu.VMEM((N_HEADS * HEAD_DIM, N_HEADS * HEAD_DIM), F32)

    def full(a):
        return pl.BlockSpec(a.shape, lambda i, j: (0,) * a.ndim)

    return pl.pallas_call(
        functools.partial(_layer_kernel, layer, n_chunks),
        grid=(b, l // tl),
        in_specs=[pl.BlockSpec((None, tl, d), lambda i, j: (i, j, 0))] + [full(a) for a in params],
        out_specs=pl.BlockSpec((None, tl, d), lambda i, j: (i, j, 0)),
        out_shape=jax.ShapeDtypeStruct(h.shape, h.dtype),
        scratch_shapes=[
            pltpu.VMEM((tl + TAIL, PROJ_COLS), F32),
            pltpu.VMEM((tl, D_MODEL), BF16),
            bd_state,
            bd_state,
            bd_state,
            bd_state,
        ],
        compiler_params=pltpu.CompilerParams(
            dimension_semantics=("arbitrary", "arbitrary"),
            vmem_limit_bytes=V7X_VMEM_LIMIT_BYTES),
        name=f"hybrid_layer{layer}",
    )(h, *params)


W_IN_PIECES = (
    ((0, 1536),),
    ((1544, 2312),),
    ((2316, 3212),),
    ((3212, 4236),),
    ((1536, 1544), (2312, 2316), (2312, 2316)),
)


def _repack_kernel(w_ref, *out_refs):
    w = w_ref[...]
    for pieces, o_ref in zip(W_IN_PIECES, out_refs):
        cols = [w[:, a:b] for a, b in pieces]
        used = sum(b - a for a, b in pieces)
        if used < o_ref.shape[1]:
            cols.append(jnp.zeros((w.shape[0], o_ref.shape[1] - used), w.dtype))
        o_ref[...] = (cols[0] if len(cols) == 1 else jnp.concatenate(cols, axis=1)).astype(o_ref.dtype)


def _split_w_in(w_in, layer):
    _, d, cols = w_in.shape
    widths = [-(-sum(b - a for a, b in p) // 128) * 128 for p in W_IN_PIECES]
    assert sum(widths) == PROJ_COLS and d % CAST_ROWS == 0
    return pl.pallas_call(
        _repack_kernel, grid=(d // CAST_ROWS,),
        in_specs=[pl.BlockSpec((None, CAST_ROWS, cols), lambda i: (layer, i, 0))],
        out_specs=[pl.BlockSpec((CAST_ROWS, w), lambda i: (i, 0)) for w in widths],
        out_shape=[jax.ShapeDtypeStruct((d, w), BF16) for w in widths], name="repack_w_in",
    )(w_in)


def _constants():
    ri = lax.broadcasted_iota(jnp.int32, (CHUNK, CHUNK), 0)
    ci = lax.broadcasted_iota(jnp.int32, (CHUNK, CHUNK), 1)
    ltri = (ri >= ci).astype(BF16)
    r2 = lax.broadcasted_iota(jnp.int32, (GROUP_W, GROUP_W), 0)
    c2 = lax.broadcasted_iota(jnp.int32, (GROUP_W, GROUP_W), 1)
    bdm = ((r2 // HEAD_DIM) == (c2 // HEAD_DIM)).astype(BF16)
    gw = GROUP_W // SSM_GROUPS
    grpm = ((r2 // gw) == (c2 // gw)).astype(F32)
    er = lax.broadcasted_iota(jnp.int32, (SMALL_W, 4 * GROUP_W), 0)
    ec = lax.broadcasted_iota(jnp.int32, (SMALL_W, 4 * GROUP_W), 1)
    eexp = (er == ec // HEAD_DIM).astype(BF16)
    levels = []
    for lev in range(N_LEVELS):
        s = 2 ** lev
        levels.append(((ri // (2 * s)) == (ci // (2 * s))) & (((ri // s) % 2) == 1) & (((ci // s) % 2) == 0))
    masks = jnp.stack([jnp.tile(m.astype(F32), (1, N_HEADS)) for m in [ri >= ci, ri > ci, ri == ci] + levels])
    return ltri, bdm, bdm.astype(F32), grpm, eexp, masks


def kernel(x, pre_norm_g, w_in, hgrn_lower_bounds, hgrn_norm_g, gdn_conv_w, gdn_a_log, gdn_dt_bias, gdn_norm_g,
           ssm_conv_w, ssm_conv_b, ssm_a_log, ssm_dt_bias, ssm_d, ssm_norm_g, rwkv_mu, rwkv_w0, rwkv_w_up,
           rwkv_a0, rwkv_a_up, rwkv_k_k, rwkv_k_a, rwkv_r_k, rwkv_ln_w, rwkv_ln_b, w_out, post_norm_g):
    depth = w_in.shape[0]
    consts = _constants()
    zeros4 = jnp.zeros((N_HEADS,), F32)
    ones4 = jnp.ones((N_HEADS,), F32)
    pad = jnp.zeros((SMALL_W - 4 * N_HEADS,), F32)
    h = x
    for l in range(depth):
        vecs = jnp.stack([
            jnp.tile(hgrn_norm_g[l], N_HEADS), jnp.tile(gdn_norm_g[l], N_HEADS),
            jnp.repeat(ssm_d[l], HEAD_DIM), ssm_norm_g[l], rwkv_w0[l], rwkv_a0[l], rwkv_k_k[l],
            rwkv_k_a[l], rwkv_r_k[l], rwkv_ln_w[l], rwkv_ln_b[l]] + [hgrn_lower_bounds[i] for i in range(depth)])
        small = jnp.stack([
            jnp.concatenate([zeros4, gdn_dt_bias[l], ssm_dt_bias[l], ssm_dt_bias[l], pad]),
            jnp.concatenate([zeros4, gdn_a_log[l], zeros4, ssm_a_log[l], pad]),
            jnp.concatenate([zeros4, zeros4, ones4, zeros4, pad])])
        h = _layer_call(
            l, h, pre_norm_g[l][None, :], *_split_w_in(w_in, l), w_out[l].astype(BF16), post_norm_g[l][None, :],
            vecs.astype(F32), small.astype(F32), gdn_conv_w[l], ssm_conv_w[l], ssm_conv_b[l][None, :],
            rwkv_mu[l][None, :], rwkv_w_up[l].astype(BF16), rwkv_a_up[l].astype(BF16), *consts)
    return h
```

```python
import functools

import jax
import jax.numpy as jnp
from jax import lax
from jax.experimental import pallas as pl
from jax.experimental.pallas import tpu as pltpu

F32 = jnp.float32
BF16 = jnp.bfloat16

D_MODEL = 1024
GROUP_W = 256
HEAD_DIM = 64
N_HEADS = 4
CHUNK = 64
SUBLANES = 8
LANES = 128
CONV_K = 4
SSM_STATE = 128
SSM_GROUPS = 2
LOW_RANK = 64
NORM_EPS = 1e-6
L2_EPS = 1e-6
RWKV_GN_EPS = 64e-5

HG_OFF = 0
GDN_OFF = 768
SSM_OFF = 1536
RWKV_OFF = 2304
RWKV_W = 3 * GROUP_W + 2 * LOW_RANK
GATE_OFF = 3200
SMALL_OFF = 4224
SMALL_W = 128
PROJ_COLS = 4352

TAIL = 8
TIME_TILE = 512
PROJ_ROWS = 256
PROJ_COLS_PER_STAGE = 512
CAST_ROWS = 256
V7X_VMEM_LIMIT_BYTES = 52 * 1024 * 1024

(V_HG_G, V_GDN_G, V_SSM_D, V_SSM_G, V_W0, V_A0, V_KK, V_KA, V_RK, V_LNW, V_LNB) = range(11)
V_LB = 11

M_CAUSAL, M_STRICT, M_EYE, M_LEVEL0 = 0, 1, 2, 3
N_LEVELS = 6
INVERSE_SPLIT = 3


def _sigmoid(x):
    return 0.5 * jnp.tanh(0.5 * x) + 0.5


def _silu(x):
    h = 0.5 * x
    return h + h * jnp.tanh(h)


def _split(a):
    hi = a.astype(BF16)
    lo = (a - hi.astype(F32)).astype(BF16)
    return hi, lo


def _dot(a, b):
    return jnp.dot(a, b, preferred_element_type=F32)


def _mm(a, b):
    return _dot(a.astype(BF16), b.astype(BF16))


def _mm_nt(a, b):
    return lax.dot_general(a.astype(BF16), b.astype(BF16), (((1,), (1,)), ((), ())),
                           preferred_element_type=F32)


def _mm_tn(a, b):
    return lax.dot_general(a.astype(BF16), b.astype(BF16), (((0,), (0,)), ((), ())),
                           preferred_element_type=F32)


def _mm_const_l(c, b):
    hi, lo = _split(b)
    return _dot(c, hi) + _dot(c, lo)


def _mm_const_r(a, c):
    hi, lo = _split(a)
    n = a.shape[0]
    t = _dot(jnp.concatenate([hi, lo], axis=0), c)
    return t[0:n] + t[n:2 * n]


def _layer_kernel(layer, n_chunks,
                  x_ref, gpre_ref, win0_ref, win1_ref, win2_ref, win3_ref, win4_ref,
                  wout_ref, gpost_ref, vec_ref, small_ref,
                  gconv_ref, sconvw_ref, sconvb_ref, mu_ref, wup_ref, aup_ref,
                  ltri_ref, bdm_ref, bdmf_ref, grpmf_ref, eexp_ref, masks_ref,
                  o_ref,
                  p_ref, y_ref, s_hg, s_gdn, s_ssm, s_rwkv):
    t_idx = pl.program_id(1)
    tl = n_chunks * CHUNK
    win_refs = (win0_ref, win1_ref, win2_ref, win3_ref, win4_ref)
    assert sum(w.shape[1] for w in win_refs) == PROJ_COLS

    @pl.when(t_idx == 0)
    def _():
        p_ref[0:TAIL, :] = jnp.zeros((TAIL, PROJ_COLS), F32)
        s_hg[...] = jnp.zeros_like(s_hg)
        s_gdn[...] = jnp.zeros_like(s_gdn)
        s_ssm[...] = jnp.zeros_like(s_ssm)
        s_rwkv[...] = jnp.zeros_like(s_rwkv)

    @pl.when(t_idx != 0)
    def _():
        p_ref[0:TAIL, :] = p_ref[tl:tl + TAIL, :]

    rows_per_block = min(PROJ_ROWS, tl)
    assert tl % rows_per_block == 0 and rows_per_block % CHUNK == 0
    n_blocks = tl // rows_per_block
    chunks_per_block = rows_per_block // CHUNK

    def in_projection(j):
        rows = pl.ds(j * rows_per_block, rows_per_block)
        h = x_ref[rows, :]
        u = h * lax.rsqrt(jnp.mean(h * h, axis=-1, keepdims=True) + NORM_EPS) * gpre_ref[...]
        ub = u.astype(BF16)
        yield
        col = 0
        for w_ref in win_refs:
            for c0 in range(0, w_ref.shape[1], PROJ_COLS_PER_STAGE):
                c1 = min(c0 + PROJ_COLS_PER_STAGE, w_ref.shape[1])
                p_ref[pl.ds(TAIL + j * rows_per_block, rows_per_block), col + c0:col + c1] = _dot(ub, w_ref[:, c0:c1])
                yield
            col += w_ref.shape[1]

    def out_projection(j):
        rows = pl.ds(j * rows_per_block, rows_per_block)
        yb = y_ref[rows, :]
        half = D_MODEL // 2
        out_lo = _dot(yb, wout_ref[:, 0:half])
        yield
        out_hi = _dot(yb, wout_ref[:, half:D_MODEL])
        yield
        ms = (jnp.sum(out_lo * out_lo, axis=-1, keepdims=True)
              + jnp.sum(out_hi * out_hi, axis=-1, keepdims=True)) * (1.0 / D_MODEL)
        scale = lax.rsqrt(ms + NORM_EPS)
        o_ref[rows, 0:half] = x_ref[rows, 0:half] + out_lo * scale * gpost_ref[:, 0:half]
        o_ref[rows, half:D_MODEL] = x_ref[rows, half:D_MODEL] + out_hi * scale * gpost_ref[:, half:D_MODEL]
        yield

    for _ in in_projection(0):
        pass

    sub_r = lax.broadcasted_iota(jnp.int32, (SUBLANES, GROUP_W), 0)
    lane_s = lax.broadcasted_iota(jnp.int32, (1, SMALL_W), 1)

    def vec(i):
        return vec_ref[i:i + 1, :]

    def mask(i):
        return masks_ref[i]

    zero_tile = jnp.zeros((CHUNK, LANES), BF16)

    def bd(y, width=HEAD_DIM):
        yb = y.astype(BF16)
        rows = []
        for hd in range(N_HEADS):
            lo = (hd * HEAD_DIM // width) * width
            tiles = []
            for t0 in range(0, GROUP_W, LANES):
                first, last = max(lo, t0), min(lo + width, t0 + LANES)
                if first >= last:
                    tiles.append(zero_tile)
                elif last - first == LANES:
                    tiles.append(yb[:, t0:t0 + LANES])
                else:
                    assert width == HEAD_DIM == CHUNK
                    tiles.append(yb[:, t0:t0 + LANES] * bdm_ref[hd * CHUNK:(hd + 1) * CHUNK, t0:t0 + LANES])
            rows.append(jnp.concatenate(tiles, axis=1))
        return jnp.concatenate(rows, axis=0)

    def head_sum(v):
        return _mm(v, bdm_ref[...])

    def mm_bd(l, r):
        return _mm(l, bd(r))

    def tri_inverse_levels(a, first, last):
        ns = a["systems"]
        for lev in range(first, last):
            m = mask(M_LEVEL0 + lev)
            if lev == 0:
                a["inverses"] = [mask(M_EYE) - m * n for n in ns]
                continue
            ws = [mm_bd(m * n, x) for n, x in zip(ns, a["inverses"])]
            yield
            a["inverses"] = [x - mm_bd(x, w) for x, w in zip(a["inverses"], ws)]
            yield

    depth = vec_ref.shape[0] - V_LB
    lbs = [vec(V_LB + l) for l in range(depth)]
    lb_max = functools.reduce(jnp.maximum, lbs)
    lb_exp = [jnp.exp(v - lb_max) for v in lbs]
    lb_den = functools.reduce(lambda a, b: a + b, lb_exp)
    sm = [e / lb_den for e in lb_exp]
    lb = functools.reduce(lambda a, b: a + b, sm[:layer + 1]) - sm[0]

    small_bias = small_ref[0:1, :]
    plain = small_ref[2:3, :]
    small_mult = plain - (1.0 - plain) * jnp.exp(small_ref[1:2, :])

    def phase_a(c, a):
        r0 = c * CHUNK
        cur = pl.ds(r0 + TAIL, CHUNK)

        def causal_conv(off, w_ref):
            y = None
            for j in range(CONV_K):
                x_j = p_ref[pl.ds(r0 + TAIL - (CONV_K - 1) + j, CHUNK), off:off + 3 * GROUP_W]
                y = w_ref[j:j + 1, :] * x_j if y is None else y + w_ref[j:j + 1, :] * x_j
            return y
        causal = mask(M_CAUSAL)
        strict = mask(M_STRICT)

        gates = p_ref[cur, GATE_OFF:GATE_OFF + 4 * GROUP_W]

        small = p_ref[cur, SMALL_OFF:SMALL_OFF + SMALL_W]
        small_act = jnp.where(lane_s < N_HEADS, _sigmoid(small),
                              jax.nn.softplus(small + small_bias) * small_mult)
        expd = _mm_const_r(small_act, eexp_ref[...])
        beta_e = expd[:, 0:GROUP_W]
        g_gdn = expd[:, GROUP_W:2 * GROUP_W]
        dt_e = expd[:, 2 * GROUP_W:3 * GROUP_W]
        g_ssd = expd[:, 3 * GROUP_W:4 * GROUP_W]
        yield

        hg = p_ref[cur, HG_OFF:HG_OFF + 3 * GROUP_W]
        q_a = _silu(hg[:, 0:GROUP_W]) * (HEAD_DIM ** -0.5)
        f_a = lb + (1.0 - lb) * _sigmoid(hg[:, GROUP_W:2 * GROUP_W])
        k_a = 1.0 - f_a
        v_a = hg[:, 2 * GROUP_W:3 * GROUP_W]
        yield

        xc = _silu(causal_conv(GDN_OFF, gconv_ref))
        q_b = xc[:, 0:GROUP_W]
        k_b = xc[:, GROUP_W:2 * GROUP_W]
        v_b = xc[:, 2 * GROUP_W:3 * GROUP_W]
        yield

        xc = _silu(causal_conv(SSM_OFF, sconvw_ref) + sconvb_ref[...])
        x_c = xc[:, 0:GROUP_W]
        b_c = xc[:, GROUP_W:2 * GROUP_W]
        c_c = xc[:, 2 * GROUP_W:3 * GROUP_W]
        yield

        xcur = p_ref[cur, RWKV_OFF:RWKV_OFF + RWKV_W]
        xr = xcur + (p_ref[pl.ds(r0 + TAIL - 1, CHUNK), RWKV_OFF:RWKV_OFF + RWKV_W] - xcur) * mu_ref[...]
        r_d = xr[:, 0:GROUP_W]
        k_d = xr[:, GROUP_W:2 * GROUP_W]
        v_d = xr[:, 2 * GROUP_W:3 * GROUP_W]
        wd = xr[:, 3 * GROUP_W:3 * GROUP_W + LOW_RANK]
        ad = xr[:, 3 * GROUP_W + LOW_RANK:3 * GROUP_W + 2 * LOW_RANK]
        w_d = -jax.nn.softplus(-(vec(V_W0) + _mm(jnp.tanh(wd), wup_ref[...]))) - 0.5
        lw = -jnp.exp(w_d)
        a_d = _sigmoid(vec(V_A0) + _mm(ad, aup_ref[...]))
        kk = k_d * vec(V_KK)
        k_d = k_d * (1.0 + (a_d - 1.0) * vec(V_KA))
        yield

        sq = head_sum(jnp.concatenate([q_b * q_b, k_b * k_b, kk * kk, q_a * k_a], axis=0))
        q_b = q_b * lax.rsqrt(sq[0:CHUNK] + L2_EPS) * (HEAD_DIM ** -0.5)
        k_b = k_b * lax.rsqrt(sq[CHUNK:2 * CHUNK] + L2_EPS)
        kk = kk * lax.rsqrt(sq[2 * CHUNK:3 * CHUNK] + L2_EPS)
        yield

        cums = _mm_const_l(ltri_ref[...], jnp.concatenate(
            [jnp.log(f_a), lw, g_gdn, g_ssd, g_gdn * strict, g_ssd * strict], axis=1))
        cum_a = cums[:, 0:GROUP_W]
        cum_d = cums[:, GROUP_W:2 * GROUP_W]
        cum_b = cums[:, 2 * GROUP_W:3 * GROUP_W]
        cum_c = cums[:, 3 * GROUP_W:4 * GROUP_W]
        gam_b = causal * jnp.exp(jnp.minimum(cums[:, 4 * GROUP_W:5 * GROUP_W], 0.0))
        gam_c = causal * jnp.exp(jnp.minimum(cums[:, 5 * GROUP_W:6 * GROUP_W], 0.0))
        yield

        kb = k_b * beta_e
        kq = _mm_nt(jnp.concatenate([kb, q_b], axis=0), bd(k_b))
        n_b = strict * kq[0:CHUNK] * gam_b
        qk_b = causal * kq[CHUNK:2 * CHUNK] * gam_b
        yield

        p_inc = jnp.exp(cum_d)
        p_inv = jnp.exp(-cum_d)
        p_last = p_inc[CHUNK - 1:CHUNK, :]
        a_t = -kk * jnp.exp(cum_d - lw)
        b_t = kk * a_d * p_inv
        k_t = k_d * p_inv
        r_t = r_d * p_inc
        ar = jnp.concatenate([a_t, r_t], axis=0)
        arb = _mm_nt(ar, bd(b_t))
        ark = _mm_nt(ar, bd(k_t))
        ab = strict * arb[0:CHUNK]
        rb = causal * arb[CHUNK:2 * CHUNK]
        ak = strict * ark[0:CHUNK]
        rk = causal * ark[CHUNK:2 * CHUNK]
        a.update(rows=pl.ds(r0, CHUNK), systems=[n_b, -ab], beta_e=beta_e, q_b=q_b, k_b=k_b, v_b=v_b, kb=kb,
                 cum_b=cum_b, qk_b=qk_b, ar=ar, b_t=b_t, k_t=k_t, p_last=p_last, ak=ak, rb=rb, rk=rk, v_d=v_d,
                 r_d=r_d, k_d=k_d, gate_b=gates[:, GROUP_W:2 * GROUP_W], gate_d=gates[:, 3 * GROUP_W:4 * GROUP_W])
        yield

        cum_last = cum_a[CHUNK - 1:CHUNK, :]
        q_in = q_a * jnp.exp(cum_a)
        k_out = k_a * jnp.exp(cum_last - cum_a)
        a_w = mask(M_EYE) * sq[3 * CHUNK:4 * CHUNK] + mask(M_LEVEL0) * _mm_nt(q_a * f_a, bd(k_a))
        yield
        for lev in range(1, N_LEVELS):
            b = 2 ** lev
            groups = []
            for g in range(CHUNK // SUBLANES):
                first = g * SUBLANES
                if 2 * b >= SUBLANES:
                    r = first // (2 * b) * (2 * b) + b - 1
                    groups.append(jnp.broadcast_to(cum_a[r:r + 1, :], (SUBLANES, GROUP_W)))
                else:
                    groups.append(jnp.where(
                        sub_r < 2 * b,
                        jnp.broadcast_to(cum_a[first + b - 1:first + b, :], (SUBLANES, GROUP_W)),
                        jnp.broadcast_to(cum_a[first + 3 * b - 1:first + 3 * b, :], (SUBLANES, GROUP_W))))
            d = cum_a - jnp.concatenate(groups, axis=0)
            a_w = a_w + mask(M_LEVEL0 + lev) * _mm_nt(q_a * jnp.exp(jnp.minimum(d, 0.0)),
                                                     bd(k_a * jnp.exp(jnp.minimum(-d, 0.0))))
            yield
        st_a = s_hg[...]
        o_a = _mm(a_w, bd(v_a)) + _mm_nt(q_in, st_a)
        s_hg[...] = st_a * jnp.exp(cum_last) + bdmf_ref[...] * _mm_tn(v_a, k_out)
        yield

        cb = _mm_nt(c_c, bd(b_c, GROUP_W // SSM_GROUPS))
        xv = x_c * dt_e
        c_last = cum_c[CHUNK - 1:CHUNK, :]
        st_c = s_ssm[...]
        o_c = _mm(cb * gam_c, bd(xv)) + jnp.exp(cum_c) * _mm(c_c, st_c)
        s_ssm[...] = st_c * jnp.exp(c_last) + grpmf_ref[...] * _mm_tn(b_c, xv * jnp.exp(c_last - cum_c))
        yield

        rows = a["rows"]
        y_a = o_a * lax.rsqrt(head_sum(o_a * o_a) * (1.0 / HEAD_DIM) + NORM_EPS) * vec(V_HG_G)
        y_ref[rows, 0:GROUP_W] = (y_a * _silu(gates[:, 0:GROUP_W])).astype(y_ref.dtype)
        yv = (o_c + x_c * vec(V_SSM_D)) * _silu(gates[:, 2 * GROUP_W:3 * GROUP_W])
        gw = GROUP_W // SSM_GROUPS
        for grp in range(SSM_GROUPS):
            yg = yv[:, grp * gw:(grp + 1) * gw]
            yg = yg * lax.rsqrt(jnp.mean(yg * yg, axis=-1, keepdims=True) + NORM_EPS)
            yg = yg * vec_ref[V_SSM_G:V_SSM_G + 1, grp * gw:(grp + 1) * gw]
            y_ref[rows, 2 * GROUP_W + grp * gw:2 * GROUP_W + (grp + 1) * gw] = yg.astype(y_ref.dtype)

    def phase_b(a):
        x_b, x_d = a["inverses"]
        (rows, beta_e, q_b, k_b, v_b, kb, cum_b, qk_b, ar, b_t, k_t, p_last, ak, rb, rk, v_d, r_d, k_d) = (
            a[n] for n in ("rows", "beta_e", "q_b", "k_b", "v_b", "kb", "cum_b", "qk_b", "ar", "b_t", "k_t",
                           "p_last", "ak", "rb", "rk", "v_d", "r_d", "k_d"))
        ecum_b = jnp.exp(cum_b)
        b_last = cum_b[CHUNK - 1:CHUNK, :]
        st_b = s_gdn[...]
        st_d = s_rwkv[...]
        uu = mm_bd(x_b, v_b * beta_e)
        ww = mm_bd(x_b, kb * ecum_b)
        rhs_d = _mm_nt(ar, st_d)
        yield
        wq = _mm(jnp.concatenate([ww, q_b * ecum_b], axis=0), st_b)
        v_new = uu - wq[0:CHUNK]
        bd_vd = bd(v_d)
        rhs_u = rhs_d[0:CHUNK] + _mm(ak, bd_vd)
        yield
        u_d = mm_bd(x_d, rhs_u)
        yield
        o_b = wq[CHUNK:2 * CHUNK] + _mm(qk_b, bd(v_new))
        o_d = rhs_d[CHUNK:2 * CHUNK] + _mm(rb, bd(u_d)) + _mm(rk, bd_vd)
        yield
        bdm = bdmf_ref[...]
        s_gdn[...] = st_b * jnp.exp(b_last) + bdm * _mm_tn(k_b * jnp.exp(b_last - cum_b), v_new)
        s_rwkv[...] = st_d * p_last + bdm * _mm_tn(jnp.concatenate([u_d, v_d], axis=0),
                                                   jnp.concatenate([b_t * p_last, k_t * p_last], axis=0))
        yield

        sums = head_sum(jnp.concatenate([o_b * o_b, o_d, r_d * k_d * vec(V_RK)], axis=0))
        y_b = o_b * lax.rsqrt(sums[0:CHUNK] * (1.0 / HEAD_DIM) + NORM_EPS) * vec(V_GDN_G)
        oc = o_d - sums[CHUNK:2 * CHUNK] * (1.0 / HEAD_DIM)
        var = head_sum(oc * oc) * (1.0 / HEAD_DIM)
        y_d = oc * lax.rsqrt(var + RWKV_GN_EPS) * vec(V_LNW) + vec(V_LNB) + sums[2 * CHUNK:3 * CHUNK] * v_d
        y_ref[rows, GROUP_W:2 * GROUP_W] = (y_b * _silu(a["gate_b"])).astype(y_ref.dtype)
        y_ref[rows, 3 * GROUP_W:4 * GROUP_W] = (y_d * _silu(a["gate_d"])).astype(y_ref.dtype)

    parts = [{} for _ in range(n_chunks)]
    threads = (
        (0, 1, lambda k: phase_a(k, parts[k])),
        (1, 4, lambda k: tri_inverse_levels(parts[k], 0, INVERSE_SPLIT)),
        (2, 3, lambda k: tri_inverse_levels(parts[k], INVERSE_SPLIT, N_LEVELS)),
        (3, 3, lambda k: phase_b(parts[k])),
    )
    n_slots = n_chunks + len(threads) - 1
    background = []
    for j in range(1, n_blocks):
        background.append(((j - 1) * chunks_per_block, j * chunks_per_block - 1, 8, in_projection(j)))
    for j in range(n_blocks):
        first = (j + 1) * chunks_per_block + len(threads) - 1
        background.append((min(first, n_slots), n_slots - 1, 4, out_projection(j)))
    for slot in range(n_slots):
        live = [(period, make(slot - lag)) for lag, period, make in threads if 0 <= slot - lag < n_chunks]
        riders = [task for task in background if task[0] <= slot <= task[1]]
        tick = 0
        while live:
            for entry in list(live):
                if tick % entry[0] == 0 and next(entry[1], "done") == "done":
                    live.remove(entry)
            for _, _, period, gen in riders:
                if tick % period == 0:
                    next(gen, None)
            tick += 1
        for _, last, _, gen in riders:
            if last == slot:
                for _ in gen:
                    pass
    for first, _, _, gen in background:
        if first >= n_slots:
            for _ in gen:
                pass


def _layer_call(layer, h, *params):
    b, l, d = h.shape
    tl = min(TIME_TILE, l)
    assert l % tl == 0 and tl % CHUNK == 0 and d == D_MODEL
    n_chunks = tl // CHUNK
    bd_state = pltpu.VMEM((N_HEADS * HEAD_DIM, N_HEADS * HEAD_DIM), F32)

    def full(a):
        return pl.BlockSpec(a.shape, lambda i, j: (0,) * a.ndim)

    return pl.pallas_call(
        functools.partial(_layer_kernel, layer, n_chunks),
        grid=(b, l // tl),
        in_specs=[pl.BlockSpec((None, tl, d), lambda i, j: (i, j, 0))] + [full(a) for a in params],
        out_specs=pl.BlockSpec((None, tl, d), lambda i, j: (i, j, 0)),
        out_shape=jax.ShapeDtypeStruct(h.shape, h.dtype),
        scratch_shapes=[
            pltpu.VMEM((tl + TAIL, PROJ_COLS), F32),
            pltpu.VMEM((tl, D_MODEL), BF16),
            bd_state,
            bd_state,
            bd_state,
            bd_state,
        ],
        compiler_params=pltpu.CompilerParams(
            dimension_semantics=("arbitrary", "arbitrary"),
            vmem_limit_bytes=V7X_VMEM_LIMIT_BYTES),
        name=f"hybrid_layer{layer}",
    )(h, *params)


W_IN_PIECES = (
    ((0, 1536),),
    ((1544, 2312),),
    ((2316, 3212),),
    ((3212, 4236),),
    ((1536, 1544), (2312, 2316), (2312, 2316)),
)


def _repack_kernel(w_ref, *out_refs):
    w = w_ref[...]
    for pieces, o_ref in zip(W_IN_PIECES, out_refs):
        cols = [w[:, a:b] for a, b in pieces]
        used = sum(b - a for a, b in pieces)
        if used < o_ref.shape[1]:
            cols.append(jnp.zeros((w.shape[0], o_ref.shape[1] - used), w.dtype))
        o_ref[...] = (cols[0] if len(cols) == 1 else jnp.concatenate(cols, axis=1)).astype(o_ref.dtype)


def _split_w_in(w_in, layer):
    _, d, cols = w_in.shape
    widths = [-(-sum(b - a for a, b in p) // 128) * 128 for p in W_IN_PIECES]
    assert sum(widths) == PROJ_COLS and d % CAST_ROWS == 0
    return pl.pallas_call(
        _repack_kernel, grid=(d // CAST_ROWS,),
        in_specs=[pl.BlockSpec((None, CAST_ROWS, cols), lambda i: (layer, i, 0))],
        out_specs=[pl.BlockSpec((CAST_ROWS, w), lambda i: (i, 0)) for w in widths],
        out_shape=[jax.ShapeDtypeStruct((d, w), BF16) for w in widths], name="repack_w_in",
    )(w_in)


def _constants():
    ri = lax.broadcasted_iota(jnp.int32, (CHUNK, CHUNK), 0)
    ci = lax.broadcasted_iota(jnp.int32, (CHUNK, CHUNK), 1)
    ltri = (ri >= ci).astype(BF16)
    r2 = lax.broadcasted_iota(jnp.int32, (GROUP_W, GROUP_W), 0)
    c2 = lax.broadcasted_iota(jnp.int32, (GROUP_W, GROUP_W), 1)
    bdm = ((r2 // HEAD_DIM) == (c2 // HEAD_DIM)).astype(BF16)
    gw = GROUP_W // SSM_GROUPS
    grpm = ((r2 // gw) == (c2 // gw)).astype(F32)
    er = lax.broadcasted_iota(jnp.int32, (SMALL_W, 4 * GROUP_W), 0)
    ec = lax.broadcasted_iota(jnp.int32, (SMALL_W, 4 * GROUP_W), 1)
    eexp = (er == ec // HEAD_DIM).astype(BF16)
    levels = []
    for lev in range(N_LEVELS):
        s = 2 ** lev
        levels.append(((ri // (2 * s)) == (ci // (2 * s))) & (((ri // s) % 2) == 1) & (((ci // s) % 2) == 0))
    masks = jnp.stack([jnp.tile(m.astype(F32), (1, N_HEADS)) for m in [ri >= ci, ri > ci, ri == ci] + levels])
    return ltri, bdm, bdm.astype(F32), grpm, eexp, masks


def kernel(x, pre_norm_g, w_in, hgrn_lower_bounds, hgrn_norm_g, gdn_conv_w, gdn_a_log, gdn_dt_bias, gdn_norm_g,
           ssm_conv_w, ssm_conv_b, ssm_a_log, ssm_dt_bias, ssm_d, ssm_norm_g, rwkv_mu, rwkv_w0, rwkv_w_up,
           rwkv_a0, rwkv_a_up, rwkv_k_k, rwkv_k_a, rwkv_r_k, rwkv_ln_w, rwkv_ln_b, w_out, post_norm_g):
    depth = w_in.shape[0]
    consts = _constants()
    zeros4 = jnp.zeros((N_HEADS,), F32)
    ones4 = jnp.ones((N_HEADS,), F32)
    pad = jnp.zeros((SMALL_W - 4 * N_HEADS,), F32)
    h = x
    for l in range(depth):
        vecs = jnp.stack([
            jnp.tile(hgrn_norm_g[l], N_HEADS), jnp.tile(gdn_norm_g[l], N_HEADS),
            jnp.repeat(ssm_d[l], HEAD_DIM), ssm_norm_g[l], rwkv_w0[l], rwkv_a0[l], rwkv_k_k[l],
            rwkv_k_a[l], rwkv_r_k[l], rwkv_ln_w[l], rwkv_ln_b[l]] + [hgrn_lower_bounds[i] for i in range(depth)])
        small = jnp.stack([
            jnp.concatenate([zeros4, gdn_dt_bias[l], ssm_dt_bias[l], ssm_dt_bias[l], pad]),
            jnp.concatenate([zeros4, gdn_a_log[l], zeros4, ssm_a_log[l], pad]),
            jnp.concatenate([zeros4, zeros4, ones4, zeros4, pad])])
        h = _layer_call(
            l, h, pre_norm_g[l][None, :], *_split_w_in(w_in, l), w_out[l].astype(BF16), post_norm_g[l][None, :],
            vecs.astype(F32), small.astype(F32), gdn_conv_w[l], ssm_conv_w[l], ssm_conv_b[l][None, :],
            rwkv_mu[l][None, :], rwkv_w_up[l].astype(BF16), rwkv_a_up[l].astype(BF16), *consts)
    return h
```

```python
import functools

import jax
import jax.numpy as jnp
from jax import lax
from jax.experimental import pallas as pl
from jax.experimental.pallas import tpu as pltpu

F32 = jnp.float32
BF16 = jnp.bfloat16

D_MODEL = 1024
GROUP_W = 256
HEAD_DIM = 64
N_HEADS = 4
CHUNK = 64
SUBLANES = 8
LANES = 128
CONV_K = 4
SSM_STATE = 128
SSM_GROUPS = 2
LOW_RANK = 64
NORM_EPS = 1e-6
L2_EPS = 1e-6
RWKV_GN_EPS = 64e-5

HG_OFF = 0
GDN_OFF = 768
SSM_OFF = 1536
RWKV_OFF = 2304
RWKV_W = 3 * GROUP_W + 2 * LOW_RANK
GATE_OFF = 3200
SMALL_OFF = 4224
SMALL_W = 128
PROJ_COLS = 4352

TAIL = 8
TIME_TILE = 512
PROJ_ROWS = 256
PROJ_COLS_PER_STAGE = 256
CAST_ROWS = 256
V7X_VMEM_LIMIT_BYTES = 52 * 1024 * 1024

(V_HG_G, V_GDN_G, V_SSM_D, V_SSM_G, V_W0, V_A0, V_KK, V_KA, V_RK, V_LNW, V_LNB) = range(11)
V_LB = 11

M_CAUSAL, M_STRICT, M_EYE, M_LEVEL0 = 0, 1, 2, 3
N_LEVELS = 6
INVERSE_SPLIT = 3


def _sigmoid(x):
    return 0.5 * jnp.tanh(0.5 * x) + 0.5


def _silu(x):
    h = 0.5 * x
    return h + h * jnp.tanh(h)


def _split(a):
    hi = a.astype(BF16)
    lo = (a - hi.astype(F32)).astype(BF16)
    return hi, lo


def _dot(a, b):
    return jnp.dot(a, b, preferred_element_type=F32)


def _mm(a, b):
    return _dot(a.astype(BF16), b.astype(BF16))


def _mm_nt(a, b):
    return lax.dot_general(a.astype(BF16), b.astype(BF16), (((1,), (1,)), ((), ())),
                           preferred_element_type=F32)


def _mm_tn(a, b):
    return lax.dot_general(a.astype(BF16), b.astype(BF16), (((0,), (0,)), ((), ())),
                           preferred_element_type=F32)


def _mm_const_l(c, b):
    hi, lo = _split(b)
    return _dot(c, hi) + _dot(c, lo)


def _mm_const_r(a, c):
    hi, lo = _split(a)
    n = a.shape[0]
    t = _dot(jnp.concatenate([hi, lo], axis=0), c)
    return t[0:n] + t[n:2 * n]


def _layer_kernel(layer, n_chunks,
                  x_ref, xnext_ref, gpre_ref, win0_ref, win1_ref, win2_ref, win3_ref, win4_ref,
                  wout_ref, gpost_ref, vec_ref, small_ref,
                  gconv_ref, sconvw_ref, sconvb_ref, mu_ref, wup_ref, aup_ref,
                  ltri_ref, bdm_ref, bdmf_ref, grpmf_ref, eexp_ref, masks_ref,
                  o_ref,
                  p_ref, y_ref, s_hg, s_gdn, s_ssm, s_rwkv):
    t_idx = pl.program_id(1)
    tl = n_chunks * CHUNK
    win_refs = (win0_ref, win1_ref, win2_ref, win3_ref, win4_ref)
    assert sum(w.shape[1] for w in win_refs) == PROJ_COLS

    @pl.when(t_idx == 0)
    def _():
        p_ref[0:TAIL, :] = jnp.zeros((TAIL, PROJ_COLS), F32)
        s_hg[...] = jnp.zeros_like(s_hg)
        s_gdn[...] = jnp.zeros_like(s_gdn)
        s_ssm[...] = jnp.zeros_like(s_ssm)
        s_rwkv[...] = jnp.zeros_like(s_rwkv)

    rows_per_block = min(PROJ_ROWS, tl)
    assert tl % rows_per_block == 0 and rows_per_block % CHUNK == 0
    n_blocks = tl // rows_per_block
    chunks_per_block = rows_per_block // CHUNK

    def in_projection(j, next_tile=False):
        if next_tile:
            p_ref[0:TAIL, :] = p_ref[tl:tl + TAIL, :]
        h = xnext_ref[...] if next_tile else x_ref[pl.ds(j * rows_per_block, rows_per_block), :]
        u = h * lax.rsqrt(jnp.mean(h * h, axis=-1, keepdims=True) + NORM_EPS) * gpre_ref[...]
        ub = u.astype(BF16)
        yield
        col = 0
        for w_ref in win_refs:
            for c0 in range(0, w_ref.shape[1], PROJ_COLS_PER_STAGE):
                c1 = min(c0 + PROJ_COLS_PER_STAGE, w_ref.shape[1])
                p_ref[pl.ds(TAIL + j * rows_per_block, rows_per_block), col + c0:col + c1] = _dot(ub, w_ref[:, c0:c1])
                yield
            col += w_ref.shape[1]

    def out_projection(j):
        rows = pl.ds(j * rows_per_block, rows_per_block)
        yb = y_ref[rows, :]
        half = D_MODEL // 2
        out_lo = _dot(yb, wout_ref[:, 0:half])
        yield
        out_hi = _dot(yb, wout_ref[:, half:D_MODEL])
        yield
        ms = (jnp.sum(out_lo * out_lo, axis=-1, keepdims=True)
              + jnp.sum(out_hi * out_hi, axis=-1, keepdims=True)) * (1.0 / D_MODEL)
        scale = lax.rsqrt(ms + NORM_EPS)
        o_ref[rows, 0:half] = x_ref[rows, 0:half] + out_lo * scale * gpost_ref[:, 0:half]
        o_ref[rows, half:D_MODEL] = x_ref[rows, half:D_MODEL] + out_hi * scale * gpost_ref[:, half:D_MODEL]
        yield

    @pl.when((pl.program_id(0) == 0) & (t_idx == 0))
    def _():
        for _ in in_projection(0):
            pass

    sub_r = lax.broadcasted_iota(jnp.int32, (SUBLANES, GROUP_W), 0)
    lane_s = lax.broadcasted_iota(jnp.int32, (1, SMALL_W), 1)

    def vec(i):
        return vec_ref[i:i + 1, :]

    def mask(i):
        return masks_ref[i]

    zero_tile = jnp.zeros((CHUNK, LANES), BF16)

    def bd(y, width=HEAD_DIM):
        yb = y.astype(BF16)
        rows = []
        for hd in range(N_HEADS):
            lo = (hd * HEAD_DIM // width) * width
            tiles = []
            for t0 in range(0, GROUP_W, LANES):
                first, last = max(lo, t0), min(lo + width, t0 + LANES)
                if first >= last:
                    tiles.append(zero_tile)
                elif last - first == LANES:
                    tiles.append(yb[:, t0:t0 + LANES])
                else:
                    assert width == HEAD_DIM == CHUNK
                    tiles.append(yb[:, t0:t0 + LANES] * bdm_ref[hd * CHUNK:(hd + 1) * CHUNK, t0:t0 + LANES])
            rows.append(jnp.concatenate(tiles, axis=1))
        return jnp.concatenate(rows, axis=0)

    def head_sum(v):
        return _mm(v, bdm_ref[...])

    def mm_bd(l, r):
        return _mm(l, bd(r))

    def tri_inverse_levels(a, first, last):
        ns = a["systems"]
        for lev in range(first, last):
            m = mask(M_LEVEL0 + lev)
            if lev == 0:
                a["inverses"] = [mask(M_EYE) - m * n for n in ns]
                continue
            ws = [mm_bd(m * n, x) for n, x in zip(ns, a["inverses"])]
            yield
            a["inverses"] = [x - mm_bd(x, w) for x, w in zip(a["inverses"], ws)]
            yield

    depth = vec_ref.shape[0] - V_LB
    lbs = [vec(V_LB + l) for l in range(depth)]
    lb_max = functools.reduce(jnp.maximum, lbs)
    lb_exp = [jnp.exp(v - lb_max) for v in lbs]
    lb_den = functools.reduce(lambda a, b: a + b, lb_exp)
    sm = [e / lb_den for e in lb_exp]
    lb = functools.reduce(lambda a, b: a + b, sm[:layer + 1]) - sm[0]

    small_bias = small_ref[0:1, :]
    plain = small_ref[2:3, :]
    small_mult = plain - (1.0 - plain) * jnp.exp(small_ref[1:2, :])

    def phase_a(c, a):
        r0 = c * CHUNK
        cur = pl.ds(r0 + TAIL, CHUNK)

        def causal_conv(off, w_ref):
            y = None
            for j in range(CONV_K):
                x_j = p_ref[pl.ds(r0 + TAIL - (CONV_K - 1) + j, CHUNK), off:off + 3 * GROUP_W]
                y = w_ref[j:j + 1, :] * x_j if y is None else y + w_ref[j:j + 1, :] * x_j
            return y
        causal = mask(M_CAUSAL)
        strict = mask(M_STRICT)

        gates = p_ref[cur, GATE_OFF:GATE_OFF + 4 * GROUP_W]

        small = p_ref[cur, SMALL_OFF:SMALL_OFF + SMALL_W]
        small_act = jnp.where(lane_s < N_HEADS, _sigmoid(small),
                              jax.nn.softplus(small + small_bias) * small_mult)
        expd = _mm_const_r(small_act, eexp_ref[...])
        beta_e = expd[:, 0:GROUP_W]
        g_gdn = expd[:, GROUP_W:2 * GROUP_W]
        dt_e = expd[:, 2 * GROUP_W:3 * GROUP_W]
        g_ssd = expd[:, 3 * GROUP_W:4 * GROUP_W]
        yield

        hg = p_ref[cur, HG_OFF:HG_OFF + 3 * GROUP_W]
        q_a = _silu(hg[:, 0:GROUP_W]) * (HEAD_DIM ** -0.5)
        f_a = lb + (1.0 - lb) * _sigmoid(hg[:, GROUP_W:2 * GROUP_W])
        k_a = 1.0 - f_a
        v_a = hg[:, 2 * GROUP_W:3 * GROUP_W]
        yield

        xc = _silu(causal_conv(GDN_OFF, gconv_ref))
        q_b = xc[:, 0:GROUP_W]
        k_b = xc[:, GROUP_W:2 * GROUP_W]
        v_b = xc[:, 2 * GROUP_W:3 * GROUP_W]
        yield

        xc = _silu(causal_conv(SSM_OFF, sconvw_ref) + sconvb_ref[...])
        x_c = xc[:, 0:GROUP_W]
        b_c = xc[:, GROUP_W:2 * GROUP_W]
        c_c = xc[:, 2 * GROUP_W:3 * GROUP_W]
        yield

        xcur = p_ref[cur, RWKV_OFF:RWKV_OFF + RWKV_W]
        xr = xcur + (p_ref[pl.ds(r0 + TAIL - 1, CHUNK), RWKV_OFF:RWKV_OFF + RWKV_W] - xcur) * mu_ref[...]
        r_d = xr[:, 0:GROUP_W]
        k_d = xr[:, GROUP_W:2 * GROUP_W]
        v_d = xr[:, 2 * GROUP_W:3 * GROUP_W]
        wd = xr[:, 3 * GROUP_W:3 * GROUP_W + LOW_RANK]
        ad = xr[:, 3 * GROUP_W + LOW_RANK:3 * GROUP_W + 2 * LOW_RANK]
        w_d = -jax.nn.softplus(-(vec(V_W0) + _mm(jnp.tanh(wd), wup_ref[...]))) - 0.5
        lw = -jnp.exp(w_d)
        a_d = _sigmoid(vec(V_A0) + _mm(ad, aup_ref[...]))
        kk = k_d * vec(V_KK)
        k_d = k_d * (1.0 + (a_d - 1.0) * vec(V_KA))
        yield

        sq = head_sum(jnp.concatenate([q_b * q_b, k_b * k_b, kk * kk, q_a * k_a], axis=0))
        q_b = q_b * lax.rsqrt(sq[0:CHUNK] + L2_EPS) * (HEAD_DIM ** -0.5)
        k_b = k_b * lax.rsqrt(sq[CHUNK:2 * CHUNK] + L2_EPS)
        kk = kk * lax.rsqrt(sq[2 * CHUNK:3 * CHUNK] + L2_EPS)
        yield

        cums = _mm_const_l(ltri_ref[...], jnp.concatenate(
            [jnp.log(f_a), lw, g_gdn, g_ssd, g_gdn * strict, g_ssd * strict], axis=1))
        cum_a = cums[:, 0:GROUP_W]
        cum_d = cums[:, GROUP_W:2 * GROUP_W]
        cum_b = cums[:, 2 * GROUP_W:3 * GROUP_W]
        cum_c = cums[:, 3 * GROUP_W:4 * GROUP_W]
        gam_b = causal * jnp.exp(jnp.minimum(cums[:, 4 * GROUP_W:5 * GROUP_W], 0.0))
        gam_c = causal * jnp.exp(jnp.minimum(cums[:, 5 * GROUP_W:6 * GROUP_W], 0.0))
        yield

        kb = k_b * beta_e
        kq = _mm_nt(jnp.concatenate([kb, q_b], axis=0), bd(k_b))
        n_b = strict * kq[0:CHUNK] * gam_b
        qk_b = causal * kq[CHUNK:2 * CHUNK] * gam_b
        yield

        p_inc = jnp.exp(cum_d)
        p_inv = jnp.exp(-cum_d)
        p_last = p_inc[CHUNK - 1:CHUNK, :]
        a_t = -kk * jnp.exp(cum_d - lw)
        b_t = kk * a_d * p_inv
        k_t = k_d * p_inv
        r_t = r_d * p_inc
        ar = jnp.concatenate([a_t, r_t], axis=0)
        arb = _mm_nt(ar, bd(b_t))
        ark = _mm_nt(ar, bd(k_t))
        ab = strict * arb[0:CHUNK]
        rb = causal * arb[CHUNK:2 * CHUNK]
        ak = strict * ark[0:CHUNK]
        rk = causal * ark[CHUNK:2 * CHUNK]
        a.update(rows=pl.ds(r0, CHUNK), systems=[n_b, -ab], beta_e=beta_e, q_b=q_b, k_b=k_b, v_b=v_b, kb=kb,
                 cum_b=cum_b, qk_b=qk_b, ar=ar, b_t=b_t, k_t=k_t, p_last=p_last, ak=ak, rb=rb, rk=rk, v_d=v_d,
                 r_d=r_d, k_d=k_d, gate_b=gates[:, GROUP_W:2 * GROUP_W], gate_d=gates[:, 3 * GROUP_W:4 * GROUP_W])
        yield

        cum_last = cum_a[CHUNK - 1:CHUNK, :]
        q_in = q_a * jnp.exp(cum_a)
        k_out = k_a * jnp.exp(cum_last - cum_a)
        a_w = mask(M_EYE) * sq[3 * CHUNK:4 * CHUNK] + mask(M_LEVEL0) * _mm_nt(q_a * f_a, bd(k_a))
        yield
        for lev in range(1, N_LEVELS):
            b = 2 ** lev
            groups = []
            for g in range(CHUNK // SUBLANES):
                first = g * SUBLANES
                if 2 * b >= SUBLANES:
                    r = first // (2 * b) * (2 * b) + b - 1
                    groups.append(jnp.broadcast_to(cum_a[r:r + 1, :], (SUBLANES, GROUP_W)))
                else:
                    groups.append(jnp.where(
                        sub_r < 2 * b,
                        jnp.broadcast_to(cum_a[first + b - 1:first + b, :], (SUBLANES, GROUP_W)),
                        jnp.broadcast_to(cum_a[first + 3 * b - 1:first + 3 * b, :], (SUBLANES, GROUP_W))))
            d = cum_a - jnp.concatenate(groups, axis=0)
            a_w = a_w + mask(M_LEVEL0 + lev) * _mm_nt(q_a * jnp.exp(jnp.minimum(d, 0.0)),
                                                     bd(k_a * jnp.exp(jnp.minimum(-d, 0.0))))
            yield
        st_a = s_hg[...]
        o_a = _mm(a_w, bd(v_a)) + _mm_nt(q_in, st_a)
        s_hg[...] = st_a * jnp.exp(cum_last) + bdmf_ref[...] * _mm_tn(v_a, k_out)
        yield

        cb = _mm_nt(c_c, bd(b_c, GROUP_W // SSM_GROUPS))
        xv = x_c * dt_e
        c_last = cum_c[CHUNK - 1:CHUNK, :]
        st_c = s_ssm[...]
        o_c = _mm(cb * gam_c, bd(xv)) + jnp.exp(cum_c) * _mm(c_c, st_c)
        s_ssm[...] = st_c * jnp.exp(c_last) + grpmf_ref[...] * _mm_tn(b_c, xv * jnp.exp(c_last - cum_c))
        yield

        rows = a["rows"]
        y_a = o_a * lax.rsqrt(head_sum(o_a * o_a) * (1.0 / HEAD_DIM) + NORM_EPS) * vec(V_HG_G)
        y_ref[rows, 0:GROUP_W] = (y_a * _silu(gates[:, 0:GROUP_W])).astype(y_ref.dtype)
        yv = (o_c + x_c * vec(V_SSM_D)) * _silu(gates[:, 2 * GROUP_W:3 * GROUP_W])
        gw = GROUP_W // SSM_GROUPS
        for grp in range(SSM_GROUPS):
            yg = yv[:, grp * gw:(grp + 1) * gw]
            yg = yg * lax.rsqrt(jnp.mean(yg * yg, axis=-1, keepdims=True) + NORM_EPS)
            yg = yg * vec_ref[V_SSM_G:V_SSM_G + 1, grp * gw:(grp + 1) * gw]
            y_ref[rows, 2 * GROUP_W + grp * gw:2 * GROUP_W + (grp + 1) * gw] = yg.astype(y_ref.dtype)

    def phase_b(a):
        x_b, x_d = a["inverses"]
        (rows, beta_e, q_b, k_b, v_b, kb, cum_b, qk_b, ar, b_t, k_t, p_last, ak, rb, rk, v_d, r_d, k_d) = (
            a[n] for n in ("rows", "beta_e", "q_b", "k_b", "v_b", "kb", "cum_b", "qk_b", "ar", "b_t", "k_t",
                           "p_last", "ak", "rb", "rk", "v_d", "r_d", "k_d"))
        ecum_b = jnp.exp(cum_b)
        b_last = cum_b[CHUNK - 1:CHUNK, :]
        st_b = s_gdn[...]
        st_d = s_rwkv[...]
        uu = mm_bd(x_b, v_b * beta_e)
        ww = mm_bd(x_b, kb * ecum_b)
        rhs_d = _mm_nt(ar, st_d)
        yield
        wq = _mm(jnp.concatenate([ww, q_b * ecum_b], axis=0), st_b)
        v_new = uu - wq[0:CHUNK]
        bd_vd = bd(v_d)
        rhs_u = rhs_d[0:CHUNK] + _mm(ak, bd_vd)
        yield
        u_d = mm_bd(x_d, rhs_u)
        yield
        o_b = wq[CHUNK:2 * CHUNK] + _mm(qk_b, bd(v_new))
        o_d = rhs_d[CHUNK:2 * CHUNK] + _mm(rb, bd(u_d)) + _mm(rk, bd_vd)
        yield
        bdm = bdmf_ref[...]
        s_gdn[...] = st_b * jnp.exp(b_last) + bdm * _mm_tn(k_b * jnp.exp(b_last - cum_b), v_new)
        s_rwkv[...] = st_d * p_last + bdm * _mm_tn(jnp.concatenate([u_d, v_d], axis=0),
                                                   jnp.concatenate([b_t * p_last, k_t * p_last], axis=0))
        yield

        sums = head_sum(jnp.concatenate([o_b * o_b, o_d, r_d * k_d * vec(V_RK)], axis=0))
        y_b = o_b * lax.rsqrt(sums[0:CHUNK] * (1.0 / HEAD_DIM) + NORM_EPS) * vec(V_GDN_G)
        oc = o_d - sums[CHUNK:2 * CHUNK] * (1.0 / HEAD_DIM)
        var = head_sum(oc * oc) * (1.0 / HEAD_DIM)
        y_d = oc * lax.rsqrt(var + RWKV_GN_EPS) * vec(V_LNW) + vec(V_LNB) + sums[2 * CHUNK:3 * CHUNK] * v_d
        y_ref[rows, GROUP_W:2 * GROUP_W] = (y_b * _silu(a["gate_b"])).astype(y_ref.dtype)
        y_ref[rows, 3 * GROUP_W:4 * GROUP_W] = (y_d * _silu(a["gate_d"])).astype(y_ref.dtype)

    parts = [{} for _ in range(n_chunks)]
    threads = (
        (0, 1, lambda k: phase_a(k, parts[k])),
        (1, 4, lambda k: tri_inverse_levels(parts[k], 0, INVERSE_SPLIT)),
        (2, 3, lambda k: tri_inverse_levels(parts[k], INVERSE_SPLIT, N_LEVELS)),
        (3, 3, lambda k: phase_b(parts[k])),
    )
    n_slots = n_chunks + len(threads) - 1
    background = []
    for j in range(1, n_blocks):
        background.append(((j - 1) * chunks_per_block, j * chunks_per_block - 1, 4, in_projection(j)))
    for j in range(n_blocks):
        first = (j + 1) * chunks_per_block + len(threads) - 1
        background.append((min(first, n_slots), n_slots - 1, 4, out_projection(j)))
    background.append((chunks_per_block, n_slots - 1, 6, in_projection(0, next_tile=True)))
    for slot in range(n_slots):
        live = [(period, make(slot - lag)) for lag, period, make in threads if 0 <= slot - lag < n_chunks]
        riders = [task for task in background if task[0] <= slot <= task[1]]
        tick = 0
        while live:
            for entry in list(live):
                if tick % entry[0] == 0 and next(entry[1], "done") == "done":
                    live.remove(entry)
            for _, _, period, gen in riders:
                if tick % period == 0:
                    next(gen, None)
            tick += 1
        for _, last, _, gen in riders:
            if last == slot:
                for _ in gen:
                    pass
    for first, _, _, gen in background:
        if first >= n_slots:
            for _ in gen:
                pass


def _layer_call(layer, h, *params):
    b, l, d = h.shape
    tl = min(TIME_TILE, l)
    assert l % tl == 0 and tl % CHUNK == 0 and d == D_MODEL
    n_chunks = tl // CHUNK
    bd_state = pltpu.VMEM((N_HEADS * HEAD_DIM, N_HEADS * HEAD_DIM), F32)

    def full(a):
        return pl.BlockSpec(a.shape, lambda i, j: (0,) * a.ndim)

    rows_next = min(PROJ_ROWS, tl)
    n_tiles = l // tl

    def next_block(i, j):
        wraps = j + 1 == n_tiles
        return (jnp.where(wraps, jnp.minimum(i + 1, b - 1), i), jnp.where(wraps, 0, (j + 1) * (tl // rows_next)), 0)

    return pl.pallas_call(
        functools.partial(_layer_kernel, layer, n_chunks),
        grid=(b, l // tl),
        in_specs=[pl.BlockSpec((None, tl, d), lambda i, j: (i, j, 0)),
                  pl.BlockSpec((None, rows_next, d), next_block)] + [full(a) for a in params],
        out_specs=pl.BlockSpec((None, tl, d), lambda i, j: (i, j, 0)),
        out_shape=jax.ShapeDtypeStruct(h.shape, h.dtype),
        scratch_shapes=[
            pltpu.VMEM((tl + TAIL, PROJ_COLS), F32),
            pltpu.VMEM((tl, D_MODEL), BF16),
            bd_state,
            bd_state,
            bd_state,
            bd_state,
        ],
        compiler_params=pltpu.CompilerParams(
            dimension_semantics=("arbitrary", "arbitrary"),
            vmem_limit_bytes=V7X_VMEM_LIMIT_BYTES),
        name=f"hybrid_layer{layer}",
    )(h, h, *params)


W_IN_PIECES = (
    ((0, 1536),),
    ((1544, 2312),),
    ((2316, 3212),),
    ((3212, 4236),),
    ((1536, 1544), (2312, 2316), (2312, 2316)),
)


def _repack_kernel(w_ref, *out_refs):
    w = w_ref[...]
    for pieces, o_ref in zip(W_IN_PIECES, out_refs):
        cols = [w[:, a:b] for a, b in pieces]
        used = sum(b - a for a, b in pieces)
        if used < o_ref.shape[1]:
            cols.append(jnp.zeros((w.shape[0], o_ref.shape[1] - used), w.dtype))
        o_ref[...] = (cols[0] if len(cols) == 1 else jnp.concatenate(cols, axis=1)).astype(o_ref.dtype)


def _split_w_in(w_in, layer):
    _, d, cols = w_in.shape
    widths = [-(-sum(b - a for a, b in p) // 128) * 128 for p in W_IN_PIECES]
    assert sum(widths) == PROJ_COLS and d % CAST_ROWS == 0
    return pl.pallas_call(
        _repack_kernel, grid=(d // CAST_ROWS,),
        in_specs=[pl.BlockSpec((None, CAST_ROWS, cols), lambda i: (layer, i, 0))],
        out_specs=[pl.BlockSpec((CAST_ROWS, w), lambda i: (i, 0)) for w in widths],
        out_shape=[jax.ShapeDtypeStruct((d, w), BF16) for w in widths], name="repack_w_in",
    )(w_in)


def _constants():
    ri = lax.broadcasted_iota(jnp.int32, (CHUNK, CHUNK), 0)
    ci = lax.broadcasted_iota(jnp.int32, (CHUNK, CHUNK), 1)
    ltri = (ri >= ci).astype(BF16)
    r2 = lax.broadcasted_iota(jnp.int32, (GROUP_W, GROUP_W), 0)
    c2 = lax.broadcasted_iota(jnp.int32, (GROUP_W, GROUP_W), 1)
    bdm = ((r2 // HEAD_DIM) == (c2 // HEAD_DIM)).astype(BF16)
    gw = GROUP_W // SSM_GROUPS
    grpm = ((r2 // gw) == (c2 // gw)).astype(F32)
    er = lax.broadcasted_iota(jnp.int32, (SMALL_W, 4 * GROUP_W), 0)
    ec = lax.broadcasted_iota(jnp.int32, (SMALL_W, 4 * GROUP_W), 1)
    eexp = (er == ec // HEAD_DIM).astype(BF16)
    levels = []
    for lev in range(N_LEVELS):
        s = 2 ** lev
        levels.append(((ri // (2 * s)) == (ci // (2 * s))) & (((ri // s) % 2) == 1) & (((ci // s) % 2) == 0))
    masks = jnp.stack([jnp.tile(m.astype(F32), (1, N_HEADS)) for m in [ri >= ci, ri > ci, ri == ci] + levels])
    return ltri, bdm, bdm.astype(F32), grpm, eexp, masks


def kernel(x, pre_norm_g, w_in, hgrn_lower_bounds, hgrn_norm_g, gdn_conv_w, gdn_a_log, gdn_dt_bias, gdn_norm_g,
           ssm_conv_w, ssm_conv_b, ssm_a_log, ssm_dt_bias, ssm_d, ssm_norm_g, rwkv_mu, rwkv_w0, rwkv_w_up,
           rwkv_a0, rwkv_a_up, rwkv_k_k, rwkv_k_a, rwkv_r_k, rwkv_ln_w, rwkv_ln_b, w_out, post_norm_g):
    depth = w_in.shape[0]
    consts = _constants()
    zeros4 = jnp.zeros((N_HEADS,), F32)
    ones4 = jnp.ones((N_HEADS,), F32)
    pad = jnp.zeros((SMALL_W - 4 * N_HEADS,), F32)
    h = x
    for l in range(depth):
        vecs = jnp.stack([
            jnp.tile(hgrn_norm_g[l], N_HEADS), jnp.tile(gdn_norm_g[l], N_HEADS),
            jnp.repeat(ssm_d[l], HEAD_DIM), ssm_norm_g[l], rwkv_w0[l], rwkv_a0[l], rwkv_k_k[l],
            rwkv_k_a[l], rwkv_r_k[l], rwkv_ln_w[l], rwkv_ln_b[l]] + [hgrn_lower_bounds[i] for i in range(depth)])
        small = jnp.stack([
            jnp.concatenate([zeros4, gdn_dt_bias[l], ssm_dt_bias[l], ssm_dt_bias[l], pad]),
            jnp.concatenate([zeros4, gdn_a_log[l], zeros4, ssm_a_log[l], pad]),
            jnp.concatenate([zeros4, zeros4, ones4, zeros4, pad])])
        h = _layer_call(
            l, h, pre_norm_g[l][None, :], *_split_w_in(w_in, l), w_out[l].astype(BF16), post_norm_g[l][None, :],
            vecs.astype(F32), small.astype(F32), gdn_conv_w[l], ssm_conv_w[l], ssm_conv_b[l][None, :],
            rwkv_mu[l][None, :], rwkv_w_up[l].astype(BF16), rwkv_a_up[l].astype(BF16), *consts)
    return h
```

```python
import functools

import jax
import jax.numpy as jnp
from jax import lax
from jax.experimental import pallas as pl
from jax.experimental.pallas import tpu as pltpu

F32 = jnp.float32
BF16 = jnp.bfloat16

D_MODEL = 1024
GROUP_W = 256
HEAD_DIM = 64
N_HEADS = 4
CHUNK = 64
SUBLANES = 8
LANES = 128
CONV_K = 4
SSM_STATE = 128
SSM_GROUPS = 2
LOW_RANK = 64
NORM_EPS = 1e-6
L2_EPS = 1e-6
RWKV_GN_EPS = 64e-5

HG_OFF = 0
GDN_OFF = 768
SSM_OFF = 1536
RWKV_OFF = 2304
RWKV_W = 3 * GROUP_W + 2 * LOW_RANK
GATE_OFF = 3200
SMALL_OFF = 4224
SMALL_W = 128
PROJ_COLS = 4352

TAIL = 8
TIME_TILE = 512
PROJ_ROWS = 256
PROJ_COLS_PER_STAGE = 256
CAST_ROWS = 256
V7X_VMEM_LIMIT_BYTES = 52 * 1024 * 1024

(V_HG_G, V_GDN_G, V_SSM_D, V_SSM_G, V_W0, V_A0, V_KK, V_KA, V_RK, V_LNW, V_LNB) = range(11)
V_LB = 11

M_CAUSAL, M_STRICT, M_EYE, M_LEVEL0 = 0, 1, 2, 3
N_LEVELS = 6
INVERSE_SPLIT = 3
SPLIT = "split"
PIPE_PERIODS = (1, 2, 4, 3, 3)
RIDER_PERIODS = (4, 4, 6)


def _sigmoid(x):
    return 0.5 * jnp.tanh(0.5 * x) + 0.5


def _silu(x):
    h = 0.5 * x
    return h + h * jnp.tanh(h)


def _split(a):
    hi = a.astype(BF16)
    lo = (a - hi.astype(F32)).astype(BF16)
    return hi, lo


def _dot(a, b):
    return jnp.dot(a, b, preferred_element_type=F32)


def _mm(a, b):
    return _dot(a.astype(BF16), b.astype(BF16))


def _mm_nt(a, b):
    return lax.dot_general(a.astype(BF16), b.astype(BF16), (((1,), (1,)), ((), ())),
                           preferred_element_type=F32)


def _mm_tn(a, b):
    return lax.dot_general(a.astype(BF16), b.astype(BF16), (((0,), (0,)), ((), ())),
                           preferred_element_type=F32)


def _mm_const_l(c, b):
    hi, lo = _split(b)
    return _dot(c, hi) + _dot(c, lo)


def _mm_const_r(a, c):
    hi, lo = _split(a)
    n = a.shape[0]
    t = _dot(jnp.concatenate([hi, lo], axis=0), c)
    return t[0:n] + t[n:2 * n]


def _layer_kernel(layer, n_chunks,
                  x_ref, xnext_ref, gpre_ref, win0_ref, win1_ref, win2_ref, win3_ref, win4_ref,
                  wout_ref, gpost_ref, vec_ref, small_ref,
                  gconv_ref, sconvw_ref, sconvb_ref, mu_ref, wup_ref, aup_ref,
                  ltri_ref, bdm_ref, bdmf_ref, grpmf_ref, eexp_ref, masks_ref,
                  o_ref,
                  p_ref, y_ref, s_hg, s_gdn, s_ssm, s_rwkv):
    t_idx = pl.program_id(1)
    tl = n_chunks * CHUNK
    win_refs = (win0_ref, win1_ref, win2_ref, win3_ref, win4_ref)
    assert sum(w.shape[1] for w in win_refs) == PROJ_COLS

    @pl.when(t_idx == 0)
    def _():
        p_ref[0:TAIL, :] = jnp.zeros((TAIL, PROJ_COLS), F32)
        s_hg[...] = jnp.zeros_like(s_hg)
        s_gdn[...] = jnp.zeros_like(s_gdn)
        s_ssm[...] = jnp.zeros_like(s_ssm)
        s_rwkv[...] = jnp.zeros_like(s_rwkv)

    rows_per_block = min(PROJ_ROWS, tl)
    assert tl % rows_per_block == 0 and rows_per_block % CHUNK == 0
    n_blocks = tl // rows_per_block
    chunks_per_block = rows_per_block // CHUNK

    def in_projection(j, next_tile=False):
        if next_tile:
            p_ref[0:TAIL, :] = p_ref[tl:tl + TAIL, :]
        h = xnext_ref[...] if next_tile else x_ref[pl.ds(j * rows_per_block, rows_per_block), :]
        u = h * lax.rsqrt(jnp.mean(h * h, axis=-1, keepdims=True) + NORM_EPS) * gpre_ref[...]
        ub = u.astype(BF16)
        yield
        col = 0
        for w_ref in win_refs:
            for c0 in range(0, w_ref.shape[1], PROJ_COLS_PER_STAGE):
                c1 = min(c0 + PROJ_COLS_PER_STAGE, w_ref.shape[1])
                p_ref[pl.ds(TAIL + j * rows_per_block, rows_per_block), col + c0:col + c1] = _dot(ub, w_ref[:, c0:c1])
                yield
            col += w_ref.shape[1]

    def out_projection(j):
        rows = pl.ds(j * rows_per_block, rows_per_block)
        yb = y_ref[rows, :]
        half = D_MODEL // 2
        out_lo = _dot(yb, wout_ref[:, 0:half])
        yield
        out_hi = _dot(yb, wout_ref[:, half:D_MODEL])
        yield
        ms = (jnp.sum(out_lo * out_lo, axis=-1, keepdims=True)
              + jnp.sum(out_hi * out_hi, axis=-1, keepdims=True)) * (1.0 / D_MODEL)
        scale = lax.rsqrt(ms + NORM_EPS)
        o_ref[rows, 0:half] = x_ref[rows, 0:half] + out_lo * scale * gpost_ref[:, 0:half]
        o_ref[rows, half:D_MODEL] = x_ref[rows, half:D_MODEL] + out_hi * scale * gpost_ref[:, half:D_MODEL]
        yield

    @pl.when((pl.program_id(0) == 0) & (t_idx == 0))
    def _():
        for _ in in_projection(0):
            pass

    sub_r = lax.broadcasted_iota(jnp.int32, (SUBLANES, GROUP_W), 0)
    lane_s = lax.broadcasted_iota(jnp.int32, (1, SMALL_W), 1)

    def vec(i):
        return vec_ref[i:i + 1, :]

    def mask(i):
        return masks_ref[i]

    zero_tile = jnp.zeros((CHUNK, LANES), BF16)

    def bd(y, width=HEAD_DIM):
        yb = y.astype(BF16)
        rows = []
        for hd in range(N_HEADS):
            lo = (hd * HEAD_DIM // width) * width
            tiles = []
            for t0 in range(0, GROUP_W, LANES):
                first, last = max(lo, t0), min(lo + width, t0 + LANES)
                if first >= last:
                    tiles.append(zero_tile)
                elif last - first == LANES:
                    tiles.append(yb[:, t0:t0 + LANES])
                else:
                    assert width == HEAD_DIM == CHUNK
                    tiles.append(yb[:, t0:t0 + LANES] * bdm_ref[hd * CHUNK:(hd + 1) * CHUNK, t0:t0 + LANES])
            rows.append(jnp.concatenate(tiles, axis=1))
        return jnp.concatenate(rows, axis=0)

    def head_sum(v):
        return _mm(v, bdm_ref[...])

    def mm_bd(l, r):
        return _mm(l, bd(r))

    def tri_inverse_levels(a, first, last):
        ns = a["systems"]
        for lev in range(first, last):
            m = mask(M_LEVEL0 + lev)
            if lev == 0:
                a["inverses"] = [mask(M_EYE) - m * n for n in ns]
                continue
            ws = [mm_bd(m * n, x) for n, x in zip(ns, a["inverses"])]
            yield
            a["inverses"] = [x - mm_bd(x, w) for x, w in zip(a["inverses"], ws)]
            yield

    depth = vec_ref.shape[0] - V_LB
    lbs = [vec(V_LB + l) for l in range(depth)]
    lb_max = functools.reduce(jnp.maximum, lbs)
    lb_exp = [jnp.exp(v - lb_max) for v in lbs]
    lb_den = functools.reduce(lambda a, b: a + b, lb_exp)
    sm = [e / lb_den for e in lb_exp]
    lb = functools.reduce(lambda a, b: a + b, sm[:layer + 1]) - sm[0]

    small_bias = small_ref[0:1, :]
    plain = small_ref[2:3, :]
    small_mult = plain - (1.0 - plain) * jnp.exp(small_ref[1:2, :])

    def phase_a(c, a):
        r0 = c * CHUNK
        cur = pl.ds(r0 + TAIL, CHUNK)

        def causal_conv(off, w_ref):
            y = None
            for j in range(CONV_K):
                x_j = p_ref[pl.ds(r0 + TAIL - (CONV_K - 1) + j, CHUNK), off:off + 3 * GROUP_W]
                y = w_ref[j:j + 1, :] * x_j if y is None else y + w_ref[j:j + 1, :] * x_j
            return y
        causal = mask(M_CAUSAL)
        strict = mask(M_STRICT)

        gates = p_ref[cur, GATE_OFF:GATE_OFF + 4 * GROUP_W]

        small = p_ref[cur, SMALL_OFF:SMALL_OFF + SMALL_W]
        small_act = jnp.where(lane_s < N_HEADS, _sigmoid(small),
                              jax.nn.softplus(small + small_bias) * small_mult)
        expd = _mm_const_r(small_act, eexp_ref[...])
        beta_e = expd[:, 0:GROUP_W]
        g_gdn = expd[:, GROUP_W:2 * GROUP_W]
        dt_e = expd[:, 2 * GROUP_W:3 * GROUP_W]
        g_ssd = expd[:, 3 * GROUP_W:4 * GROUP_W]
        yield

        hg = p_ref[cur, HG_OFF:HG_OFF + 3 * GROUP_W]
        q_a = _silu(hg[:, 0:GROUP_W]) * (HEAD_DIM ** -0.5)
        f_a = lb + (1.0 - lb) * _sigmoid(hg[:, GROUP_W:2 * GROUP_W])
        k_a = 1.0 - f_a
        v_a = hg[:, 2 * GROUP_W:3 * GROUP_W]
        yield

        xc = _silu(causal_conv(GDN_OFF, gconv_ref))
        q_b = xc[:, 0:GROUP_W]
        k_b = xc[:, GROUP_W:2 * GROUP_W]
        v_b = xc[:, 2 * GROUP_W:3 * GROUP_W]
        yield

        xc = _silu(causal_conv(SSM_OFF, sconvw_ref) + sconvb_ref[...])
        x_c = xc[:, 0:GROUP_W]
        b_c = xc[:, GROUP_W:2 * GROUP_W]
        c_c = xc[:, 2 * GROUP_W:3 * GROUP_W]
        yield

        xcur = p_ref[cur, RWKV_OFF:RWKV_OFF + RWKV_W]
        xr = xcur + (p_ref[pl.ds(r0 + TAIL - 1, CHUNK), RWKV_OFF:RWKV_OFF + RWKV_W] - xcur) * mu_ref[...]
        r_d = xr[:, 0:GROUP_W]
        k_d = xr[:, GROUP_W:2 * GROUP_W]
        v_d = xr[:, 2 * GROUP_W:3 * GROUP_W]
        wd = xr[:, 3 * GROUP_W:3 * GROUP_W + LOW_RANK]
        ad = xr[:, 3 * GROUP_W + LOW_RANK:3 * GROUP_W + 2 * LOW_RANK]
        w_d = -jax.nn.softplus(-(vec(V_W0) + _mm(jnp.tanh(wd), wup_ref[...]))) - 0.5
        lw = -jnp.exp(w_d)
        a_d = _sigmoid(vec(V_A0) + _mm(ad, aup_ref[...]))
        kk = k_d * vec(V_KK)
        k_d = k_d * (1.0 + (a_d - 1.0) * vec(V_KA))
        yield

        sq = head_sum(jnp.concatenate([q_b * q_b, k_b * k_b, kk * kk, q_a * k_a], axis=0))
        q_b = q_b * lax.rsqrt(sq[0:CHUNK] + L2_EPS) * (HEAD_DIM ** -0.5)
        k_b = k_b * lax.rsqrt(sq[CHUNK:2 * CHUNK] + L2_EPS)
        kk = kk * lax.rsqrt(sq[2 * CHUNK:3 * CHUNK] + L2_EPS)
        yield

        cums = _mm_const_l(ltri_ref[...], jnp.concatenate(
            [jnp.log(f_a), lw, g_gdn, g_ssd, g_gdn * strict, g_ssd * strict], axis=1))
        cum_a = cums[:, 0:GROUP_W]
        cum_d = cums[:, GROUP_W:2 * GROUP_W]
        cum_b = cums[:, 2 * GROUP_W:3 * GROUP_W]
        cum_c = cums[:, 3 * GROUP_W:4 * GROUP_W]
        gam_b = causal * jnp.exp(jnp.minimum(cums[:, 4 * GROUP_W:5 * GROUP_W], 0.0))
        gam_c = causal * jnp.exp(jnp.minimum(cums[:, 5 * GROUP_W:6 * GROUP_W], 0.0))
        yield

        kb = k_b * beta_e
        kq = _mm_nt(jnp.concatenate([kb, q_b], axis=0), bd(k_b))
        n_b = strict * kq[0:CHUNK] * gam_b
        qk_b = kq[CHUNK:2 * CHUNK] * gam_b
        yield

        p_inc = jnp.exp(cum_d)
        p_inv = jnp.exp(-cum_d)
        p_last = p_inc[CHUNK - 1:CHUNK, :]
        a_t = -kk * jnp.exp(cum_d - lw)
        b_t = kk * a_d * p_inv
        k_t = k_d * p_inv
        r_t = r_d * p_inc
        ar = jnp.concatenate([a_t, r_t], axis=0)
        arb = _mm_nt(ar, bd(b_t))
        ark = _mm_nt(ar, bd(k_t))
        ab = strict * arb[0:CHUNK]
        rb = causal * arb[CHUNK:2 * CHUNK]
        ak = strict * ark[0:CHUNK]
        rk = causal * ark[CHUNK:2 * CHUNK]
        a.update(rows=pl.ds(r0, CHUNK), systems=[n_b, -ab], beta_e=beta_e, q_b=q_b, k_b=k_b, v_b=v_b, kb=kb,
                 cum_b=cum_b, qk_b=qk_b, ar=ar, b_t=b_t, k_t=k_t, p_last=p_last, ak=ak, rb=rb, rk=rk, v_d=v_d,
                 r_d=r_d, k_d=k_d, gate_b=gates[:, GROUP_W:2 * GROUP_W], gate_d=gates[:, 3 * GROUP_W:4 * GROUP_W])
        yield SPLIT

        cum_last = cum_a[CHUNK - 1:CHUNK, :]
        q_in = q_a * jnp.exp(cum_a)
        k_out = k_a * jnp.exp(cum_last - cum_a)
        a_w = mask(M_EYE) * sq[3 * CHUNK:4 * CHUNK] + mask(M_LEVEL0) * _mm_nt(q_a * f_a, bd(k_a))
        yield
        for lev in range(1, N_LEVELS):
            b = 2 ** lev
            groups = []
            for g in range(CHUNK // SUBLANES):
                first = g * SUBLANES
                if 2 * b >= SUBLANES:
                    r = first // (2 * b) * (2 * b) + b - 1
                    groups.append(jnp.broadcast_to(cum_a[r:r + 1, :], (SUBLANES, GROUP_W)))
                else:
                    groups.append(jnp.where(
                        sub_r < 2 * b,
                        jnp.broadcast_to(cum_a[first + b - 1:first + b, :], (SUBLANES, GROUP_W)),
                        jnp.broadcast_to(cum_a[first + 3 * b - 1:first + 3 * b, :], (SUBLANES, GROUP_W))))
            e = jnp.exp(-jnp.abs(cum_a - jnp.concatenate(groups, axis=0)))
            a_w = a_w + mask(M_LEVEL0 + lev) * _mm_nt(q_a * e, bd(k_a * e))
            yield
        st_a = s_hg[...]
        o_a = _mm(a_w, bd(v_a)) + _mm_nt(q_in, st_a)
        s_hg[...] = st_a * jnp.exp(cum_last) + bdmf_ref[...] * _mm_tn(v_a, k_out)
        yield

        cb = _mm_nt(c_c, bd(b_c, GROUP_W // SSM_GROUPS))
        xv = x_c * dt_e
        c_last = cum_c[CHUNK - 1:CHUNK, :]
        st_c = s_ssm[...]
        o_c = _mm(cb * gam_c, bd(xv)) + jnp.exp(cum_c) * _mm(c_c, st_c)
        s_ssm[...] = st_c * jnp.exp(c_last) + grpmf_ref[...] * _mm_tn(b_c, xv * jnp.exp(c_last - cum_c))
        yield

        rows = a["rows"]
        y_a = o_a * lax.rsqrt(head_sum(o_a * o_a) * (1.0 / HEAD_DIM) + NORM_EPS) * vec(V_HG_G)
        y_ref[rows, 0:GROUP_W] = (y_a * _silu(gates[:, 0:GROUP_W])).astype(y_ref.dtype)
        yv = (o_c + x_c * vec(V_SSM_D)) * _silu(gates[:, 2 * GROUP_W:3 * GROUP_W])
        gw = GROUP_W // SSM_GROUPS
        for grp in range(SSM_GROUPS):
            yg = yv[:, grp * gw:(grp + 1) * gw]
            yg = yg * lax.rsqrt(jnp.mean(yg * yg, axis=-1, keepdims=True) + NORM_EPS)
            yg = yg * vec_ref[V_SSM_G:V_SSM_G + 1, grp * gw:(grp + 1) * gw]
            y_ref[rows, 2 * GROUP_W + grp * gw:2 * GROUP_W + (grp + 1) * gw] = yg.astype(y_ref.dtype)

    def phase_b(a):
        x_b, x_d = a["inverses"]
        (rows, beta_e, q_b, k_b, v_b, kb, cum_b, qk_b, ar, b_t, k_t, p_last, ak, rb, rk, v_d, r_d, k_d) = (
            a[n] for n in ("rows", "beta_e", "q_b", "k_b", "v_b", "kb", "cum_b", "qk_b", "ar", "b_t", "k_t",
                           "p_last", "ak", "rb", "rk", "v_d", "r_d", "k_d"))
        ecum_b = jnp.exp(cum_b)
        b_last = cum_b[CHUNK - 1:CHUNK, :]
        st_b = s_gdn[...]
        st_d = s_rwkv[...]
        uu = mm_bd(x_b, v_b * beta_e)
        ww = mm_bd(x_b, kb * ecum_b)
        rhs_d = _mm_nt(ar, st_d)
        yield
        wq = _mm(jnp.concatenate([ww, q_b * ecum_b], axis=0), st_b)
        v_new = uu - wq[0:CHUNK]
        bd_vd = bd(v_d)
        rhs_u = rhs_d[0:CHUNK] + _mm(ak, bd_vd)
        yield
        u_d = mm_bd(x_d, rhs_u)
        yield
        o_b = wq[CHUNK:2 * CHUNK] + _mm(qk_b, bd(v_new))
        o_d = rhs_d[CHUNK:2 * CHUNK] + _mm(rb, bd(u_d)) + _mm(rk, bd_vd)
        yield
        bdm = bdmf_ref[...]
        s_gdn[...] = st_b * jnp.exp(b_last) + bdm * _mm_tn(k_b * jnp.exp(b_last - cum_b), v_new)
        s_rwkv[...] = st_d * p_last + bdm * _mm_tn(jnp.concatenate([u_d, v_d], axis=0),
                                                   jnp.concatenate([b_t * p_last, k_t * p_last], axis=0))
        yield

        sums = head_sum(jnp.concatenate([o_b * o_b, o_d, r_d * k_d * vec(V_RK)], axis=0))
        y_b = o_b * lax.rsqrt(sums[0:CHUNK] * (1.0 / HEAD_DIM) + NORM_EPS) * vec(V_GDN_G)
        oc = o_d - sums[CHUNK:2 * CHUNK] * (1.0 / HEAD_DIM)
        var = head_sum(oc * oc) * (1.0 / HEAD_DIM)
        y_d = oc * lax.rsqrt(var + RWKV_GN_EPS) * vec(V_LNW) + vec(V_LNB) + sums[2 * CHUNK:3 * CHUNK] * v_d
        y_ref[rows, GROUP_W:2 * GROUP_W] = (y_b * _silu(a["gate_b"])).astype(y_ref.dtype)
        y_ref[rows, 3 * GROUP_W:4 * GROUP_W] = (y_d * _silu(a["gate_d"])).astype(y_ref.dtype)

    parts = [{} for _ in range(n_chunks)]
    a_gens = {}

    def phase_a_first(k):
        a_gens[k] = phase_a(k, parts[k])
        for marker in a_gens[k]:
            if marker is SPLIT:
                return
            yield

    threads = (
        (0, PIPE_PERIODS[0], phase_a_first),
        (1, PIPE_PERIODS[1], lambda k: a_gens[k]),
        (1, PIPE_PERIODS[2], lambda k: tri_inverse_levels(parts[k], 0, INVERSE_SPLIT)),
        (2, PIPE_PERIODS[3], lambda k: tri_inverse_levels(parts[k], INVERSE_SPLIT, N_LEVELS)),
        (3, PIPE_PERIODS[4], lambda k: phase_b(parts[k])),
    )
    max_lag = max(lag for lag, _, _ in threads)
    n_slots = n_chunks + max_lag
    background = []
    for j in range(1, n_blocks):
        background.append(((j - 1) * chunks_per_block, j * chunks_per_block - 1, RIDER_PERIODS[0], in_projection(j)))
    for j in range(n_blocks):
        first = (j + 1) * chunks_per_block + max_lag
        background.append((min(first, n_slots), n_slots - 1, RIDER_PERIODS[1], out_projection(j)))
    background.append((chunks_per_block, n_slots - 1, RIDER_PERIODS[2], in_projection(0, next_tile=True)))
    for slot in range(n_slots):
        live = [(period, make(slot - lag)) for lag, period, make in threads if 0 <= slot - lag < n_chunks]
        riders = [task for task in background if task[0] <= slot <= task[1]]
        tick = 0
        while live:
            for entry in list(live):
                if tick % entry[0] == 0 and next(entry[1], "done") == "done":
                    live.remove(entry)
            for _, _, period, gen in riders:
                if tick % period == 0:
                    next(gen, None)
            tick += 1
        for _, last, _, gen in riders:
            if last == slot:
                for _ in gen:
                    pass
    for first, _, _, gen in background:
        if first >= n_slots:
            for _ in gen:
                pass


def _layer_call(layer, h, *params):
    b, l, d = h.shape
    tl = min(TIME_TILE, l)
    assert l % tl == 0 and tl % CHUNK == 0 and d == D_MODEL
    n_chunks = tl // CHUNK
    bd_state = pltpu.VMEM((N_HEADS * HEAD_DIM, N_HEADS * HEAD_DIM), F32)

    def full(a):
        return pl.BlockSpec(a.shape, lambda i, j: (0,) * a.ndim)

    rows_next = min(PROJ_ROWS, tl)
    n_tiles = l // tl

    def next_block(i, j):
        wraps = j + 1 == n_tiles
        return (jnp.where(wraps, jnp.minimum(i + 1, b - 1), i), jnp.where(wraps, 0, (j + 1) * (tl // rows_next)), 0)

    return pl.pallas_call(
        functools.partial(_layer_kernel, layer, n_chunks),
        grid=(b, l // tl),
        in_specs=[pl.BlockSpec((None, tl, d), lambda i, j: (i, j, 0)),
                  pl.BlockSpec((None, rows_next, d), next_block)] + [full(a) for a in params],
        out_specs=pl.BlockSpec((None, tl, d), lambda i, j: (i, j, 0)),
        out_shape=jax.ShapeDtypeStruct(h.shape, h.dtype),
        scratch_shapes=[
            pltpu.VMEM((tl + TAIL, PROJ_COLS), F32),
            pltpu.VMEM((tl, D_MODEL), BF16),
            bd_state,
            bd_state,
            bd_state,
            bd_state,
        ],
        compiler_params=pltpu.CompilerParams(
            dimension_semantics=("arbitrary", "arbitrary"),
            vmem_limit_bytes=V7X_VMEM_LIMIT_BYTES),
        name=f"hybrid_layer{layer}",
    )(h, h, *params)


W_IN_PIECES = (
    ((0, 1536),),
    ((1544, 2312),),
    ((2316, 3212),),
    ((3212, 4236),),
    ((1536, 1544), (2312, 2316), (2312, 2316)),
)


def _repack_kernel(w_ref, *out_refs):
    w = w_ref[...]
    for pieces, o_ref in zip(W_IN_PIECES, out_refs):
        cols = [w[:, a:b] for a, b in pieces]
        used = sum(b - a for a, b in pieces)
        if used < o_ref.shape[1]:
            cols.append(jnp.zeros((w.shape[0], o_ref.shape[1] - used), w.dtype))
        o_ref[...] = (cols[0] if len(cols) == 1 else jnp.concatenate(cols, axis=1)).astype(o_ref.dtype)


def _split_w_in(w_in, layer):
    _, d, cols = w_in.shape
    widths = [-(-sum(b - a for a, b in p) // 128) * 128 for p in W_IN_PIECES]
    assert sum(widths) == PROJ_COLS and d % CAST_ROWS == 0
    return pl.pallas_call(
        _repack_kernel, grid=(d // CAST_ROWS,),
        in_specs=[pl.BlockSpec((None, CAST_ROWS, cols), lambda i: (layer, i, 0))],
        out_specs=[pl.BlockSpec((CAST_ROWS, w), lambda i: (i, 0)) for w in widths],
        out_shape=[jax.ShapeDtypeStruct((d, w), BF16) for w in widths], name="repack_w_in",
    )(w_in)


def _constants():
    ri = lax.broadcasted_iota(jnp.int32, (CHUNK, CHUNK), 0)
    ci = lax.broadcasted_iota(jnp.int32, (CHUNK, CHUNK), 1)
    ltri = (ri >= ci).astype(BF16)
    r2 = lax.broadcasted_iota(jnp.int32, (GROUP_W, GROUP_W), 0)
    c2 = lax.broadcasted_iota(jnp.int32, (GROUP_W, GROUP_W), 1)
    bdm = ((r2 // HEAD_DIM) == (c2 // HEAD_DIM)).astype(BF16)
    gw = GROUP_W // SSM_GROUPS
    grpm = ((r2 // gw) == (c2 // gw)).astype(F32)
    er = lax.broadcasted_iota(jnp.int32, (SMALL_W, 4 * GROUP_W), 0)
    ec = lax.broadcasted_iota(jnp.int32, (SMALL_W, 4 * GROUP_W), 1)
    eexp = (er == ec // HEAD_DIM).astype(BF16)
    levels = []
    for lev in range(N_LEVELS):
        s = 2 ** lev
        levels.append(((ri // (2 * s)) == (ci // (2 * s))) & (((ri // s) % 2) == 1) & (((ci // s) % 2) == 0))
    masks = jnp.stack([jnp.tile(m.astype(F32), (1, N_HEADS)) for m in [ri >= ci, ri > ci, ri == ci] + levels])
    return ltri, bdm, bdm.astype(F32), grpm, eexp, masks


def kernel(x, pre_norm_g, w_in, hgrn_lower_bounds, hgrn_norm_g, gdn_conv_w, gdn_a_log, gdn_dt_bias, gdn_norm_g,
           ssm_conv_w, ssm_conv_b, ssm_a_log, ssm_dt_bias, ssm_d, ssm_norm_g, rwkv_mu, rwkv_w0, rwkv_w_up,
           rwkv_a0, rwkv_a_up, rwkv_k_k, rwkv_k_a, rwkv_r_k, rwkv_ln_w, rwkv_ln_b, w_out, post_norm_g):
    depth = w_in.shape[0]
    consts = _constants()
    zeros4 = jnp.zeros((N_HEADS,), F32)
    ones4 = jnp.ones((N_HEADS,), F32)
    pad = jnp.zeros((SMALL_W - 4 * N_HEADS,), F32)
    h = x
    for l in range(depth):
        vecs = jnp.stack([
            jnp.tile(hgrn_norm_g[l], N_HEADS), jnp.tile(gdn_norm_g[l], N_HEADS),
            jnp.repeat(ssm_d[l], HEAD_DIM), ssm_norm_g[l], rwkv_w0[l], rwkv_a0[l], rwkv_k_k[l],
            rwkv_k_a[l], rwkv_r_k[l], rwkv_ln_w[l], rwkv_ln_b[l]] + [hgrn_lower_bounds[i] for i in range(depth)])
        small = jnp.stack([
            jnp.concatenate([zeros4, gdn_dt_bias[l], ssm_dt_bias[l], ssm_dt_bias[l], pad]),
            jnp.concatenate([zeros4, gdn_a_log[l], zeros4, ssm_a_log[l], pad]),
            jnp.concatenate([zeros4, zeros4, ones4, zeros4, pad])])
        h = _layer_call(
            l, h, pre_norm_g[l][None, :], *_split_w_in(w_in, l), w_out[l].astype(BF16), post_norm_g[l][None, :],
            vecs.astype(F32), small.astype(F32), gdn_conv_w[l], ssm_conv_w[l], ssm_conv_b[l][None, :],
            rwkv_mu[l][None, :], rwkv_w_up[l].astype(BF16), rwkv_a_up[l].astype(BF16), *consts)
    return h
```

```python
import functools

import jax
import jax.numpy as jnp
from jax import lax
from jax.experimental import pallas as pl
from jax.experimental.pallas import tpu as pltpu

F32 = jnp.float32
BF16 = jnp.bfloat16

D_MODEL = 1024
GROUP_W = 256
HEAD_DIM = 64
N_HEADS = 4
CHUNK = 64
SUBLANES = 8
LANES = 128
CONV_K = 4
SSM_STATE = 128
SSM_GROUPS = 2
LOW_RANK = 64
NORM_EPS = 1e-6
L2_EPS = 1e-6
RWKV_GN_EPS = 64e-5

HG_OFF = 0
GDN_OFF = 768
SSM_OFF = 1536
RWKV_OFF = 2304
RWKV_W = 3 * GROUP_W + 2 * LOW_RANK
GATE_OFF = 3200
SMALL_OFF = 4224
SMALL_W = 128
PROJ_COLS = 4352

TAIL = 8
TIME_TILE = 512
PROJ_ROWS = 256
PROJ_COLS_PER_STAGE = 256
CAST_ROWS = 256
V7X_VMEM_LIMIT_BYTES = 52 * 1024 * 1024

(V_HG_G, V_GDN_G, V_SSM_D, V_SSM_G, V_W0, V_A0, V_KK, V_KA, V_RK, V_LNW, V_LNB) = range(11)
V_LB = 11

M_CAUSAL, M_STRICT, M_EYE, M_LEVEL0 = 0, 1, 2, 3
N_LEVELS = 6
INVERSE_SPLIT = 3
SPLIT = "split"
PIPE_PERIODS = (1, 2, 2, 2, 2)
RIDER_PERIODS = (4, 4, 6)


def _sigmoid(x):
    return 0.5 * jnp.tanh(0.5 * x) + 0.5


def _silu(x):
    h = 0.5 * x
    return h + h * jnp.tanh(h)


def _split(a):
    hi = a.astype(BF16)
    lo = (a - hi.astype(F32)).astype(BF16)
    return hi, lo


def _dot(a, b):
    return jnp.dot(a, b, preferred_element_type=F32)


def _mm(a, b):
    return _dot(a.astype(BF16), b.astype(BF16))


def _mm_nt(a, b):
    return lax.dot_general(a.astype(BF16), b.astype(BF16), (((1,), (1,)), ((), ())),
                           preferred_element_type=F32)


def _mm_tn(a, b):
    return lax.dot_general(a.astype(BF16), b.astype(BF16), (((0,), (0,)), ((), ())),
                           preferred_element_type=F32)


def _mm_const_l(c, b):
    hi, lo = _split(b)
    return _dot(c, hi) + _dot(c, lo)


def _mm_const_r(a, c):
    hi, lo = _split(a)
    n = a.shape[0]
    t = _dot(jnp.concatenate([hi, lo], axis=0), c)
    return t[0:n] + t[n:2 * n]


def _layer_kernel(layer, n_chunks,
                  x_ref, xnext_ref, gpre_ref, win0_ref, win1_ref, win2_ref, win3_ref, win4_ref,
                  wout_ref, gpost_ref, vec_ref, small_ref,
                  gconv_ref, sconvw_ref, sconvb_ref, mu_ref, wup_ref, aup_ref,
                  ltri_ref, bdm_ref, bdmf_ref, grpmf_ref, eexp_ref, masks_ref,
                  o_ref,
                  p_ref, y_ref, s_hg, s_gdn, s_ssm, s_rwkv):
    t_idx = pl.program_id(1)
    tl = n_chunks * CHUNK
    win_refs = (win0_ref, win1_ref, win2_ref, win3_ref, win4_ref)
    assert sum(w.shape[1] for w in win_refs) == PROJ_COLS

    @pl.when(t_idx == 0)
    def _():
        p_ref[0:TAIL, :] = jnp.zeros((TAIL, PROJ_COLS), F32)
        s_hg[...] = jnp.zeros_like(s_hg)
        s_gdn[...] = jnp.zeros_like(s_gdn)
        s_ssm[...] = jnp.zeros_like(s_ssm)
        s_rwkv[...] = jnp.zeros_like(s_rwkv)

    rows_per_block = min(PROJ_ROWS, tl)
    assert tl % rows_per_block == 0 and rows_per_block % CHUNK == 0
    n_blocks = tl // rows_per_block
    chunks_per_block = rows_per_block // CHUNK

    def in_projection(j, next_tile=False):
        if next_tile:
            p_ref[0:TAIL, :] = p_ref[tl:tl + TAIL, :]
        h = xnext_ref[...] if next_tile else x_ref[pl.ds(j * rows_per_block, rows_per_block), :]
        u = h * lax.rsqrt(jnp.mean(h * h, axis=-1, keepdims=True) + NORM_EPS) * gpre_ref[...]
        ub = u.astype(BF16)
        yield
        col = 0
        for w_ref in win_refs:
            for c0 in range(0, w_ref.shape[1], PROJ_COLS_PER_STAGE):
                c1 = min(c0 + PROJ_COLS_PER_STAGE, w_ref.shape[1])
                p_ref[pl.ds(TAIL + j * rows_per_block, rows_per_block), col + c0:col + c1] = _dot(ub, w_ref[:, c0:c1])
                yield
            col += w_ref.shape[1]

    def out_projection(j):
        rows = pl.ds(j * rows_per_block, rows_per_block)
        yb = y_ref[rows, :]
        half = D_MODEL // 2
        out_lo = _dot(yb, wout_ref[:, 0:half])
        yield
        out_hi = _dot(yb, wout_ref[:, half:D_MODEL])
        yield
        ms = (jnp.sum(out_lo * out_lo, axis=-1, keepdims=True)
              + jnp.sum(out_hi * out_hi, axis=-1, keepdims=True)) * (1.0 / D_MODEL)
        scale = lax.rsqrt(ms + NORM_EPS)
        o_ref[rows, 0:half] = x_ref[rows, 0:half] + out_lo * scale * gpost_ref[:, 0:half]
        o_ref[rows, half:D_MODEL] = x_ref[rows, half:D_MODEL] + out_hi * scale * gpost_ref[:, half:D_MODEL]
        yield

    @pl.when((pl.program_id(0) == 0) & (t_idx == 0))
    def _():
        for _ in in_projection(0):
            pass

    sub_r = lax.broadcasted_iota(jnp.int32, (SUBLANES, GROUP_W), 0)
    lane_s = lax.broadcasted_iota(jnp.int32, (1, SMALL_W), 1)

    def vec(i):
        return vec_ref[i:i + 1, :]

    def mask(i):
        return masks_ref[i]

    zero_tile = jnp.zeros((CHUNK, LANES), BF16)

    def bd(y, width=HEAD_DIM):
        yb = y.astype(BF16)
        rows = []
        for hd in range(N_HEADS):
            lo = (hd * HEAD_DIM // width) * width
            tiles = []
            for t0 in range(0, GROUP_W, LANES):
                first, last = max(lo, t0), min(lo + width, t0 + LANES)
                if first >= last:
                    tiles.append(zero_tile)
                elif last - first == LANES:
                    tiles.append(yb[:, t0:t0 + LANES])
                else:
                    assert width == HEAD_DIM == CHUNK
                    tiles.append(yb[:, t0:t0 + LANES] * bdm_ref[hd * CHUNK:(hd + 1) * CHUNK, t0:t0 + LANES])
            rows.append(jnp.concatenate(tiles, axis=1))
        return jnp.concatenate(rows, axis=0)

    def head_sum(v):
        return _mm(v, bdm_ref[...])

    def mm_bd(l, r):
        return _mm(l, bd(r))

    def tri_inverse_levels(a, first, last):
        ns = a["systems"]
        for lev in range(first, last):
            m = mask(M_LEVEL0 + lev)
            if lev == 0:
                a["inverses"] = [mask(M_EYE) - m * n for n in ns]
                continue
            ws = [mm_bd(m * n, x) for n, x in zip(ns, a["inverses"])]
            yield
            a["inverses"] = [x - mm_bd(x, w) for x, w in zip(a["inverses"], ws)]
            yield

    depth = vec_ref.shape[0] - V_LB
    lbs = [vec(V_LB + l) for l in range(depth)]
    lb_max = functools.reduce(jnp.maximum, lbs)
    lb_exp = [jnp.exp(v - lb_max) for v in lbs]
    lb_den = functools.reduce(lambda a, b: a + b, lb_exp)
    sm = [e / lb_den for e in lb_exp]
    lb = functools.reduce(lambda a, b: a + b, sm[:layer + 1]) - sm[0]

    small_bias = small_ref[0:1, :]
    plain = small_ref[2:3, :]
    small_mult = plain - (1.0 - plain) * jnp.exp(small_ref[1:2, :])

    def phase_a(c, a):
        r0 = c * CHUNK
        cur = pl.ds(r0 + TAIL, CHUNK)

        def causal_conv(off, w_ref):
            y = None
            for j in range(CONV_K):
                x_j = p_ref[pl.ds(r0 + TAIL - (CONV_K - 1) + j, CHUNK), off:off + 3 * GROUP_W]
                y = w_ref[j:j + 1, :] * x_j if y is None else y + w_ref[j:j + 1, :] * x_j
            return y
        causal = mask(M_CAUSAL)
        strict = mask(M_STRICT)

        gates = p_ref[cur, GATE_OFF:GATE_OFF + 4 * GROUP_W]

        small = p_ref[cur, SMALL_OFF:SMALL_OFF + SMALL_W]
        small_act = jnp.where(lane_s < N_HEADS, _sigmoid(small),
                              jax.nn.softplus(small + small_bias) * small_mult)
        expd = _mm_const_r(small_act, eexp_ref[...])
        beta_e = expd[:, 0:GROUP_W]
        g_gdn = expd[:, GROUP_W:2 * GROUP_W]
        dt_e = expd[:, 2 * GROUP_W:3 * GROUP_W]
        g_ssd = expd[:, 3 * GROUP_W:4 * GROUP_W]
        yield

        hg = p_ref[cur, HG_OFF:HG_OFF + 3 * GROUP_W]
        q_a = _silu(hg[:, 0:GROUP_W]) * (HEAD_DIM ** -0.5)
        f_a = lb + (1.0 - lb) * _sigmoid(hg[:, GROUP_W:2 * GROUP_W])
        k_a = 1.0 - f_a
        v_a = hg[:, 2 * GROUP_W:3 * GROUP_W]
        yield

        xc = _silu(causal_conv(GDN_OFF, gconv_ref))
        q_b = xc[:, 0:GROUP_W]
        k_b = xc[:, GROUP_W:2 * GROUP_W]
        v_b = xc[:, 2 * GROUP_W:3 * GROUP_W]
        yield

        xc = _silu(causal_conv(SSM_OFF, sconvw_ref) + sconvb_ref[...])
        x_c = xc[:, 0:GROUP_W]
        b_c = xc[:, GROUP_W:2 * GROUP_W]
        c_c = xc[:, 2 * GROUP_W:3 * GROUP_W]
        yield

        xcur = p_ref[cur, RWKV_OFF:RWKV_OFF + RWKV_W]
        xr = xcur + (p_ref[pl.ds(r0 + TAIL - 1, CHUNK), RWKV_OFF:RWKV_OFF + RWKV_W] - xcur) * mu_ref[...]
        r_d = xr[:, 0:GROUP_W]
        k_d = xr[:, GROUP_W:2 * GROUP_W]
        v_d = xr[:, 2 * GROUP_W:3 * GROUP_W]
        wd = xr[:, 3 * GROUP_W:3 * GROUP_W + LOW_RANK]
        ad = xr[:, 3 * GROUP_W + LOW_RANK:3 * GROUP_W + 2 * LOW_RANK]
        w_d = -jax.nn.softplus(-(vec(V_W0) + _mm(jnp.tanh(wd), wup_ref[...]))) - 0.5
        lw = -jnp.exp(w_d)
        a_d = _sigmoid(vec(V_A0) + _mm(ad, aup_ref[...]))
        kk = k_d * vec(V_KK)
        k_d = k_d * (1.0 + (a_d - 1.0) * vec(V_KA))
        yield

        sq = head_sum(jnp.concatenate([q_b * q_b, k_b * k_b, kk * kk, q_a * k_a], axis=0))
        q_b = q_b * lax.rsqrt(sq[0:CHUNK] + L2_EPS) * (HEAD_DIM ** -0.5)
        k_b = k_b * lax.rsqrt(sq[CHUNK:2 * CHUNK] + L2_EPS)
        kk = kk * lax.rsqrt(sq[2 * CHUNK:3 * CHUNK] + L2_EPS)
        yield

        cums = _mm_const_l(ltri_ref[...], jnp.concatenate(
            [jnp.log(f_a), lw, g_gdn, g_ssd, g_gdn * strict, g_ssd * strict], axis=1))
        cum_a = cums[:, 0:GROUP_W]
        cum_d = cums[:, GROUP_W:2 * GROUP_W]
        cum_b = cums[:, 2 * GROUP_W:3 * GROUP_W]
        cum_c = cums[:, 3 * GROUP_W:4 * GROUP_W]
        gam_b = causal * jnp.exp(jnp.minimum(cums[:, 4 * GROUP_W:5 * GROUP_W], 0.0))
        gam_c = causal * jnp.exp(jnp.minimum(cums[:, 5 * GROUP_W:6 * GROUP_W], 0.0))
        yield

        kb = k_b * beta_e
        kq = _mm_nt(jnp.concatenate([kb, q_b], axis=0), bd(k_b))
        n_b = strict * kq[0:CHUNK] * gam_b
        qk_b = kq[CHUNK:2 * CHUNK] * gam_b
        yield

        p_inc = jnp.exp(cum_d)
        p_inv = jnp.exp(-cum_d)
        p_last = p_inc[CHUNK - 1:CHUNK, :]
        a_t = -kk * jnp.exp(cum_d - lw)
        b_t = kk * a_d * p_inv
        k_t = k_d * p_inv
        r_t = r_d * p_inc
        ar = jnp.concatenate([a_t, r_t], axis=0)
        arb = _mm_nt(ar, bd(b_t))
        ark = _mm_nt(ar, bd(k_t))
        ab = strict * arb[0:CHUNK]
        rb = causal * arb[CHUNK:2 * CHUNK]
        ak = strict * ark[0:CHUNK]
        rk = causal * ark[CHUNK:2 * CHUNK]
        a.update(rows=pl.ds(r0, CHUNK), systems=[n_b, -ab], beta_e=beta_e, q_b=q_b, k_b=k_b, v_b=v_b, kb=kb,
                 cum_b=cum_b, qk_b=qk_b, ar=ar, b_t=b_t, k_t=k_t, p_last=p_last, ak=ak, rb=rb, rk=rk, v_d=v_d,
                 r_d=r_d, k_d=k_d, gate_b=gates[:, GROUP_W:2 * GROUP_W], gate_d=gates[:, 3 * GROUP_W:4 * GROUP_W])
        yield SPLIT

        cum_last = cum_a[CHUNK - 1:CHUNK, :]
        q_in = q_a * jnp.exp(cum_a)
        k_out = k_a * jnp.exp(cum_last - cum_a)
        a_w = mask(M_EYE) * sq[3 * CHUNK:4 * CHUNK] + mask(M_LEVEL0) * _mm_nt(q_a * f_a, bd(k_a))
        yield
        for lev in range(1, N_LEVELS):
            b = 2 ** lev
            groups = []
            for g in range(CHUNK // SUBLANES):
                first = g * SUBLANES
                if 2 * b >= SUBLANES:
                    r = first // (2 * b) * (2 * b) + b - 1
                    groups.append(jnp.broadcast_to(cum_a[r:r + 1, :], (SUBLANES, GROUP_W)))
                else:
                    groups.append(jnp.where(
                        sub_r < 2 * b,
                        jnp.broadcast_to(cum_a[first + b - 1:first + b, :], (SUBLANES, GROUP_W)),
                        jnp.broadcast_to(cum_a[first + 3 * b - 1:first + 3 * b, :], (SUBLANES, GROUP_W))))
            e = jnp.exp(-jnp.abs(cum_a - jnp.concatenate(groups, axis=0)))
            a_w = a_w + mask(M_LEVEL0 + lev) * _mm_nt(q_a * e, bd(k_a * e))
            yield
        st_a = s_hg[...]
        o_a = _mm(a_w, bd(v_a)) + _mm_nt(q_in, st_a)
        s_hg[...] = st_a * jnp.exp(cum_last) + bdmf_ref[...] * _mm_tn(v_a, k_out)
        yield

        cb = _mm_nt(c_c, bd(b_c, GROUP_W // SSM_GROUPS))
        xv = x_c * dt_e
        c_last = cum_c[CHUNK - 1:CHUNK, :]
        st_c = s_ssm[...]
        o_c = _mm(cb * gam_c, bd(xv)) + jnp.exp(cum_c) * _mm(c_c, st_c)
        s_ssm[...] = st_c * jnp.exp(c_last) + grpmf_ref[...] * _mm_tn(b_c, xv * jnp.exp(c_last - cum_c))
        yield

        rows = a["rows"]
        y_a = o_a * lax.rsqrt(head_sum(o_a * o_a) * (1.0 / HEAD_DIM) + NORM_EPS) * vec(V_HG_G)
        y_ref[rows, 0:GROUP_W] = (y_a * _silu(gates[:, 0:GROUP_W])).astype(y_ref.dtype)
        yv = (o_c + x_c * vec(V_SSM_D)) * _silu(gates[:, 2 * GROUP_W:3 * GROUP_W])
        gw = GROUP_W // SSM_GROUPS
        for grp in range(SSM_GROUPS):
            yg = yv[:, grp * gw:(grp + 1) * gw]
            yg = yg * lax.rsqrt(jnp.mean(yg * yg, axis=-1, keepdims=True) + NORM_EPS)
            yg = yg * vec_ref[V_SSM_G:V_SSM_G + 1, grp * gw:(grp + 1) * gw]
            y_ref[rows, 2 * GROUP_W + grp * gw:2 * GROUP_W + (grp + 1) * gw] = yg.astype(y_ref.dtype)

    def phase_b(a):
        x_b, x_d = a["inverses"]
        (rows, beta_e, q_b, k_b, v_b, kb, cum_b, qk_b, ar, b_t, k_t, p_last, ak, rb, rk, v_d, r_d, k_d) = (
            a[n] for n in ("rows", "beta_e", "q_b", "k_b", "v_b", "kb", "cum_b", "qk_b", "ar", "b_t", "k_t",
                           "p_last", "ak", "rb", "rk", "v_d", "r_d", "k_d"))
        ecum_b = jnp.exp(cum_b)
        b_last = cum_b[CHUNK - 1:CHUNK, :]
        st_b = s_gdn[...]
        st_d = s_rwkv[...]
        uu = mm_bd(x_b, v_b * beta_e)
        ww = mm_bd(x_b, kb * ecum_b)
        rhs_d = _mm_nt(ar, st_d)
        yield
        wq = _mm(jnp.concatenate([ww, q_b * ecum_b], axis=0), st_b)
        v_new = uu - wq[0:CHUNK]
        bd_vd = bd(v_d)
        rhs_u = rhs_d[0:CHUNK] + _mm(ak, bd_vd)
        yield
        u_d = mm_bd(x_d, rhs_u)
        yield
        o_b = wq[CHUNK:2 * CHUNK] + _mm(qk_b, bd(v_new))
        o_d = rhs_d[CHUNK:2 * CHUNK] + _mm(rb, bd(u_d)) + _mm(rk, bd_vd)
        yield
        bdm = bdmf_ref[...]
        s_gdn[...] = st_b * jnp.exp(b_last) + bdm * _mm_tn(k_b * jnp.exp(b_last - cum_b), v_new)
        s_rwkv[...] = st_d * p_last + bdm * _mm_tn(jnp.concatenate([u_d, v_d], axis=0),
                                                   jnp.concatenate([b_t * p_last, k_t * p_last], axis=0))
        yield

        sums = head_sum(jnp.concatenate([o_b * o_b, o_d, r_d * k_d * vec(V_RK)], axis=0))
        y_b = o_b * lax.rsqrt(sums[0:CHUNK] * (1.0 / HEAD_DIM) + NORM_EPS) * vec(V_GDN_G)
        oc = o_d - sums[CHUNK:2 * CHUNK] * (1.0 / HEAD_DIM)
        var = head_sum(oc * oc) * (1.0 / HEAD_DIM)
        y_d = oc * lax.rsqrt(var + RWKV_GN_EPS) * vec(V_LNW) + vec(V_LNB) + sums[2 * CHUNK:3 * CHUNK] * v_d
        y_ref[rows, GROUP_W:2 * GROUP_W] = (y_b * _silu(a["gate_b"])).astype(y_ref.dtype)
        y_ref[rows, 3 * GROUP_W:4 * GROUP_W] = (y_d * _silu(a["gate_d"])).astype(y_ref.dtype)

    parts = [{} for _ in range(n_chunks)]
    a_gens = {}

    def phase_a_first(k):
        a_gens[k] = phase_a(k, parts[k])
        for marker in a_gens[k]:
            if marker is SPLIT:
                return
            yield

    threads = (
        (0, PIPE_PERIODS[0], phase_a_first),
        (1, PIPE_PERIODS[1], lambda k: a_gens[k]),
        (1, PIPE_PERIODS[2], lambda k: tri_inverse_levels(parts[k], 0, INVERSE_SPLIT)),
        (2, PIPE_PERIODS[3], lambda k: tri_inverse_levels(parts[k], INVERSE_SPLIT, N_LEVELS)),
        (3, PIPE_PERIODS[4], lambda k: phase_b(parts[k])),
    )
    max_lag = max(lag for lag, _, _ in threads)
    n_slots = n_chunks + max_lag
    background = []
    for j in range(1, n_blocks):
        background.append(((j - 1) * chunks_per_block, j * chunks_per_block - 1, RIDER_PERIODS[0], in_projection(j)))
    for j in range(n_blocks):
        first = (j + 1) * chunks_per_block + max_lag
        background.append((min(first, n_slots), n_slots - 1, RIDER_PERIODS[1], out_projection(j)))
    background.append((chunks_per_block, n_slots - 1, RIDER_PERIODS[2], in_projection(0, next_tile=True)))
    for slot in range(n_slots):
        live = [(period, make(slot - lag)) for lag, period, make in threads if 0 <= slot - lag < n_chunks]
        riders = [task for task in background if task[0] <= slot <= task[1]]
        tick = 0
        while live:
            for entry in list(live):
                if tick % entry[0] == 0 and next(entry[1], "done") == "done":
                    live.remove(entry)
            for _, _, period, gen in riders:
                if tick % period == 0:
                    next(gen, None)
            tick += 1
        for _, last, _, gen in riders:
            if last == slot:
                for _ in gen:
                    pass
    for first, _, _, gen in background:
        if first >= n_slots:
            for _ in gen:
                pass


def _layer_call(layer, h, *params):
    b, l, d = h.shape
    tl = min(TIME_TILE, l)
    assert l % tl == 0 and tl % CHUNK == 0 and d == D_MODEL
    n_chunks = tl // CHUNK
    bd_state = pltpu.VMEM((N_HEADS * HEAD_DIM, N_HEADS * HEAD_DIM), F32)

    def full(a):
        return pl.BlockSpec(a.shape, lambda i, j: (0,) * a.ndim)

    rows_next = min(PROJ_ROWS, tl)
    n_tiles = l // tl

    def next_block(i, j):
        wraps = j + 1 == n_tiles
        return (jnp.where(wraps, jnp.minimum(i + 1, b - 1), i), jnp.where(wraps, 0, (j + 1) * (tl // rows_next)), 0)

    return pl.pallas_call(
        functools.partial(_layer_kernel, layer, n_chunks),
        grid=(b, l // tl),
        in_specs=[pl.BlockSpec((None, tl, d), lambda i, j: (i, j, 0)),
                  pl.BlockSpec((None, rows_next, d), next_block)] + [full(a) for a in params],
        out_specs=pl.BlockSpec((None, tl, d), lambda i, j: (i, j, 0)),
        out_shape=jax.ShapeDtypeStruct(h.shape, h.dtype),
        scratch_shapes=[
            pltpu.VMEM((tl + TAIL, PROJ_COLS), F32),
            pltpu.VMEM((tl, D_MODEL), BF16),
            bd_state,
            bd_state,
            bd_state,
            bd_state,
        ],
        compiler_params=pltpu.CompilerParams(
            dimension_semantics=("arbitrary", "arbitrary"),
            vmem_limit_bytes=V7X_VMEM_LIMIT_BYTES),
        name=f"hybrid_layer{layer}",
    )(h, h, *params)


W_IN_PIECES = (
    ((0, 1536),),
    ((1544, 2312),),
    ((2316, 3212),),
    ((3212, 4236),),
    ((1536, 1544), (2312, 2316), (2312, 2316)),
)


def _repack_kernel(w_ref, *out_refs):
    w = w_ref[...]
    for pieces, o_ref in zip(W_IN_PIECES, out_refs):
        cols = [w[:, a:b] for a, b in pieces]
        used = sum(b - a for a, b in pieces)
        if used < o_ref.shape[1]:
            cols.append(jnp.zeros((w.shape[0], o_ref.shape[1] - used), w.dtype))
        o_ref[...] = (cols[0] if len(cols) == 1 else jnp.concatenate(cols, axis=1)).astype(o_ref.dtype)


def _split_w_in(w_in, layer):
    _, d, cols = w_in.shape
    widths = [-(-sum(b - a for a, b in p) // 128) * 128 for p in W_IN_PIECES]
    assert sum(widths) == PROJ_COLS and d % CAST_ROWS == 0
    return pl.pallas_call(
        _repack_kernel, grid=(d // CAST_ROWS,),
        in_specs=[pl.BlockSpec((None, CAST_ROWS, cols), lambda i: (layer, i, 0))],
        out_specs=[pl.BlockSpec((CAST_ROWS, w), lambda i: (i, 0)) for w in widths],
        out_shape=[jax.ShapeDtypeStruct((d, w), BF16) for w in widths], name="repack_w_in",
    )(w_in)


def _constants():
    ri = lax.broadcasted_iota(jnp.int32, (CHUNK, CHUNK), 0)
    ci = lax.broadcasted_iota(jnp.int32, (CHUNK, CHUNK), 1)
    ltri = (ri >= ci).astype(BF16)
    r2 = lax.broadcasted_iota(jnp.int32, (GROUP_W, GROUP_W), 0)
    c2 = lax.broadcasted_iota(jnp.int32, (GROUP_W, GROUP_W), 1)
    bdm = ((r2 // HEAD_DIM) == (c2 // HEAD_DIM)).astype(BF16)
    gw = GROUP_W // SSM_GROUPS
    grpm = ((r2 // gw) == (c2 // gw)).astype(F32)
    er = lax.broadcasted_iota(jnp.int32, (SMALL_W, 4 * GROUP_W), 0)
    ec = lax.broadcasted_iota(jnp.int32, (SMALL_W, 4 * GROUP_W), 1)
    eexp = (er == ec // HEAD_DIM).astype(BF16)
    levels = []
    for lev in range(N_LEVELS):
        s = 2 ** lev
        levels.append(((ri // (2 * s)) == (ci // (2 * s))) & (((ri // s) % 2) == 1) & (((ci // s) % 2) == 0))
    masks = jnp.stack([jnp.tile(m.astype(F32), (1, N_HEADS)) for m in [ri >= ci, ri > ci, ri == ci] + levels])
    return ltri, bdm, bdm.astype(F32), grpm, eexp, masks


def kernel(x, pre_norm_g, w_in, hgrn_lower_bounds, hgrn_norm_g, gdn_conv_w, gdn_a_log, gdn_dt_bias, gdn_norm_g,
           ssm_conv_w, ssm_conv_b, ssm_a_log, ssm_dt_bias, ssm_d, ssm_norm_g, rwkv_mu, rwkv_w0, rwkv_w_up,
           rwkv_a0, rwkv_a_up, rwkv_k_k, rwkv_k_a, rwkv_r_k, rwkv_ln_w, rwkv_ln_b, w_out, post_norm_g):
    depth = w_in.shape[0]
    consts = _constants()
    zeros4 = jnp.zeros((N_HEADS,), F32)
    ones4 = jnp.ones((N_HEADS,), F32)
    pad = jnp.zeros((SMALL_W - 4 * N_HEADS,), F32)
    h = x
    for l in range(depth):
        vecs = jnp.stack([
            jnp.tile(hgrn_norm_g[l], N_HEADS), jnp.tile(gdn_norm_g[l], N_HEADS),
            jnp.repeat(ssm_d[l], HEAD_DIM), ssm_norm_g[l], rwkv_w0[l], rwkv_a0[l], rwkv_k_k[l],
            rwkv_k_a[l], rwkv_r_k[l], rwkv_ln_w[l], rwkv_ln_b[l]] + [hgrn_lower_bounds[i] for i in range(depth)])
        small = jnp.stack([
            jnp.concatenate([zeros4, gdn_dt_bias[l], ssm_dt_bias[l], ssm_dt_bias[l], pad]),
            jnp.concatenate([zeros4, gdn_a_log[l], zeros4, ssm_a_log[l], pad]),
            jnp.concatenate([zeros4, zeros4, ones4, zeros4, pad])])
        h = _layer_call(
            l, h, pre_norm_g[l][None, :], *_split_w_in(w_in, l), w_out[l].astype(BF16), post_norm_g[l][None, :],
            vecs.astype(F32), small.astype(F32), gdn_conv_w[l], ssm_conv_w[l], ssm_conv_b[l][None, :],
            rwkv_mu[l][None, :], rwkv_w_up[l].astype(BF16), rwkv_a_up[l].astype(BF16), *consts)
    return h
```

```python
import functools

import jax
import jax.numpy as jnp
from jax import lax
from jax.experimental import pallas as pl
from jax.experimental.pallas import tpu as pltpu

F32 = jnp.float32
BF16 = jnp.bfloat16

D_MODEL = 1024
GROUP_W = 256
HEAD_DIM = 64
N_HEADS = 4
CHUNK = 64
SUBLANES = 8
LANES = 128
CONV_K = 4
SSM_STATE = 128
SSM_GROUPS = 2
LOW_RANK = 64
NORM_EPS = 1e-6
L2_EPS = 1e-6
RWKV_GN_EPS = 64e-5

HG_OFF = 0
GDN_OFF = 768
SSM_OFF = 1536
RWKV_OFF = 2304
RWKV_W = 3 * GROUP_W + 2 * LOW_RANK
GATE_OFF = 3200
SMALL_OFF = 4224
SMALL_W = 128
PROJ_COLS = 4352

TAIL = 8
TIME_TILE = 512
PROJ_ROWS = 256
PROJ_COLS_PER_STAGE = 256
CAST_ROWS = 256
V7X_VMEM_LIMIT_BYTES = 52 * 1024 * 1024

(V_HG_G, V_GDN_G, V_SSM_D, V_SSM_G, V_W0, V_A0, V_KK, V_KA, V_RK, V_LNW, V_LNB) = range(11)
V_LB = 11

M_CAUSAL, M_STRICT, M_EYE, M_LEVEL0 = 0, 1, 2, 3
N_LEVELS = 6
INVERSE_SPLIT = 3
SPLIT = "split"
PIPE_PERIODS = (1, 2, 2, 2, 2)
RIDER_PERIODS = (4, 4, 6)


def _sigmoid(x):
    return 0.5 * jnp.tanh(0.5 * x) + 0.5


def _silu(x):
    h = 0.5 * x
    return h + h * jnp.tanh(h)


def _split(a):
    hi = a.astype(BF16)
    lo = (a - hi.astype(F32)).astype(BF16)
    return hi, lo


def _dot(a, b):
    return jnp.dot(a, b, preferred_element_type=F32)


def _mm(a, b):
    return _dot(a.astype(BF16), b.astype(BF16))


def _mm_nt(a, b):
    return lax.dot_general(a.astype(BF16), b.astype(BF16), (((1,), (1,)), ((), ())),
                           preferred_element_type=F32)


def _mm_tn(a, b):
    return lax.dot_general(a.astype(BF16), b.astype(BF16), (((0,), (0,)), ((), ())),
                           preferred_element_type=F32)


def _mm_const_l(c, b):
    hi, lo = _split(b)
    return _dot(c, hi) + _dot(c, lo)


def _mm_const_r(a, c):
    hi, lo = _split(a)
    n = a.shape[0]
    t = _dot(jnp.concatenate([hi, lo], axis=0), c)
    return t[0:n] + t[n:2 * n]


def _layer_kernel(layer, n_chunks,
                  x_ref, xnext_ref, gpre_ref, win0_ref, win1_ref, win2_ref, win3_ref, win4_ref,
                  wout_ref, gpost_ref, vec_ref, small_ref,
                  gconv_ref, sconvw_ref, sconvb_ref, mu_ref, wup_ref, aup_ref,
                  ltri_ref, bdm_ref, bdmf_ref, grpmf_ref, eexp_ref, masks_ref,
                  o_ref,
                  p_ref, y_ref, s_hg, s_gdn, s_ssm, s_rwkv):
    t_idx = pl.program_id(1)
    tl = n_chunks * CHUNK
    win_refs = (win0_ref, win1_ref, win2_ref, win3_ref, win4_ref)
    assert sum(w.shape[1] for w in win_refs) == PROJ_COLS

    @pl.when(t_idx == 0)
    def _():
        p_ref[0:TAIL, :] = jnp.zeros((TAIL, PROJ_COLS), F32)
        s_hg[...] = jnp.zeros_like(s_hg)
        s_gdn[...] = jnp.zeros_like(s_gdn)
        s_ssm[...] = jnp.zeros_like(s_ssm)
        s_rwkv[...] = jnp.zeros_like(s_rwkv)

    rows_per_block = min(PROJ_ROWS, tl)
    assert tl % rows_per_block == 0 and rows_per_block % CHUNK == 0
    n_blocks = tl // rows_per_block
    chunks_per_block = rows_per_block // CHUNK

    def in_projection(j, next_tile=False):
        if next_tile:
            p_ref[0:TAIL, :] = p_ref[tl:tl + TAIL, :]
        h = xnext_ref[...] if next_tile else x_ref[pl.ds(j * rows_per_block, rows_per_block), :]
        u = h * lax.rsqrt(jnp.mean(h * h, axis=-1, keepdims=True) + NORM_EPS) * gpre_ref[...]
        ub = u.astype(BF16)
        yield
        col = 0
        for w_ref in win_refs:
            for c0 in range(0, w_ref.shape[1], PROJ_COLS_PER_STAGE):
                c1 = min(c0 + PROJ_COLS_PER_STAGE, w_ref.shape[1])
                p_ref[pl.ds(TAIL + j * rows_per_block, rows_per_block), col + c0:col + c1] = _dot(ub, w_ref[:, c0:c1])
                yield
            col += w_ref.shape[1]

    def out_projection(j):
        rows = pl.ds(j * rows_per_block, rows_per_block)
        yb = y_ref[rows, :]
        half = D_MODEL // 2
        out_lo = _dot(yb, wout_ref[:, 0:half])
        yield
        out_hi = _dot(yb, wout_ref[:, half:D_MODEL])
        yield
        ms = (jnp.sum(out_lo * out_lo, axis=-1, keepdims=True)
              + jnp.sum(out_hi * out_hi, axis=-1, keepdims=True)) * (1.0 / D_MODEL)
        scale = lax.rsqrt(ms + NORM_EPS)
        o_ref[rows, 0:half] = x_ref[rows, 0:half] + out_lo * scale * gpost_ref[:, 0:half]
        o_ref[rows, half:D_MODEL] = x_ref[rows, half:D_MODEL] + out_hi * scale * gpost_ref[:, half:D_MODEL]
        yield

    @pl.when((pl.program_id(0) == 0) & (t_idx == 0))
    def _():
        for _ in in_projection(0):
            pass

    sub_r = lax.broadcasted_iota(jnp.int32, (SUBLANES, GROUP_W), 0)
    lane_s = lax.broadcasted_iota(jnp.int32, (1, SMALL_W), 1)

    def vec(i):
        return vec_ref[i:i + 1, :]

    def mask(i):
        return masks_ref[i]

    zero_tile = jnp.zeros((CHUNK, LANES), BF16)

    def bd(y, width=HEAD_DIM):
        yb = y.astype(BF16)
        rows = []
        for hd in range(N_HEADS):
            lo = (hd * HEAD_DIM // width) * width
            tiles = []
            for t0 in range(0, GROUP_W, LANES):
                first, last = max(lo, t0), min(lo + width, t0 + LANES)
                if first >= last:
                    tiles.append(zero_tile)
                elif last - first == LANES:
                    tiles.append(yb[:, t0:t0 + LANES])
                else:
                    assert width == HEAD_DIM == CHUNK
                    tiles.append(yb[:, t0:t0 + LANES] * bdm_ref[hd * CHUNK:(hd + 1) * CHUNK, t0:t0 + LANES])
            rows.append(jnp.concatenate(tiles, axis=1))
        return jnp.concatenate(rows, axis=0)

    def head_sum(v):
        return _mm(v, bdm_ref[...])

    def mm_bd(l, r):
        return _mm(l, bd(r))

    def tri_inverse_levels(a, first, last):
        ns = a["systems"]
        for lev in range(first, last):
            m = mask(M_LEVEL0 + lev)
            if lev == 0:
                a["inverses"] = [mask(M_EYE) - m * n for n in ns]
                continue
            ws = [mm_bd(m * n, x) for n, x in zip(ns, a["inverses"])]
            yield
            a["inverses"] = [x - mm_bd(x, w) for x, w in zip(a["inverses"], ws)]
            yield

    depth = vec_ref.shape[0] - V_LB
    lbs = [vec(V_LB + l) for l in range(depth)]
    lb_max = functools.reduce(jnp.maximum, lbs)
    lb_exp = [jnp.exp(v - lb_max) for v in lbs]
    lb_den = functools.reduce(lambda a, b: a + b, lb_exp)
    sm = [e / lb_den for e in lb_exp]
    lb = functools.reduce(lambda a, b: a + b, sm[:layer + 1]) - sm[0]

    small_bias = small_ref[0:1, :]
    plain = small_ref[2:3, :]
    small_mult = plain - (1.0 - plain) * jnp.exp(small_ref[1:2, :])

    def phase_a(c, a):
        r0 = c * CHUNK
        cur = pl.ds(r0 + TAIL, CHUNK)

        def causal_conv(off, w_ref):
            xw = p_ref[pl.ds(r0, CHUNK + TAIL), off:off + 3 * GROUP_W]
            z = w_ref[0:1, :] * xw
            for j in range(1, CONV_K):
                z = pltpu.roll(z, 1, axis=0) + w_ref[j:j + 1, :] * xw
            return z[TAIL:TAIL + CHUNK, :]
        causal = mask(M_CAUSAL)
        strict = mask(M_STRICT)

        gates = p_ref[cur, GATE_OFF:GATE_OFF + 4 * GROUP_W]

        small = p_ref[cur, SMALL_OFF:SMALL_OFF + SMALL_W]
        small_act = jnp.where(lane_s < N_HEADS, _sigmoid(small),
                              jax.nn.softplus(small + small_bias) * small_mult)
        expd = _mm_const_r(small_act, eexp_ref[...])
        beta_e = expd[:, 0:GROUP_W]
        g_gdn = expd[:, GROUP_W:2 * GROUP_W]
        dt_e = expd[:, 2 * GROUP_W:3 * GROUP_W]
        g_ssd = expd[:, 3 * GROUP_W:4 * GROUP_W]
        yield

        hg = p_ref[cur, HG_OFF:HG_OFF + 3 * GROUP_W]
        q_a = _silu(hg[:, 0:GROUP_W]) * (HEAD_DIM ** -0.5)
        f_a = lb + (1.0 - lb) * _sigmoid(hg[:, GROUP_W:2 * GROUP_W])
        k_a = 1.0 - f_a
        v_a = hg[:, 2 * GROUP_W:3 * GROUP_W]
        yield

        xc = _silu(causal_conv(GDN_OFF, gconv_ref))
        q_b = xc[:, 0:GROUP_W]
        k_b = xc[:, GROUP_W:2 * GROUP_W]
        v_b = xc[:, 2 * GROUP_W:3 * GROUP_W]
        yield

        xc = _silu(causal_conv(SSM_OFF, sconvw_ref) + sconvb_ref[...])
        x_c = xc[:, 0:GROUP_W]
        b_c = xc[:, GROUP_W:2 * GROUP_W]
        c_c = xc[:, 2 * GROUP_W:3 * GROUP_W]
        yield

        xcur = p_ref[cur, RWKV_OFF:RWKV_OFF + RWKV_W]
        xr = xcur + (p_ref[pl.ds(r0 + TAIL - 1, CHUNK), RWKV_OFF:RWKV_OFF + RWKV_W] - xcur) * mu_ref[...]
        r_d = xr[:, 0:GROUP_W]
        k_d = xr[:, GROUP_W:2 * GROUP_W]
        v_d = xr[:, 2 * GROUP_W:3 * GROUP_W]
        wd = xr[:, 3 * GROUP_W:3 * GROUP_W + LOW_RANK]
        ad = xr[:, 3 * GROUP_W + LOW_RANK:3 * GROUP_W + 2 * LOW_RANK]
        w_d = -jax.nn.softplus(-(vec(V_W0) + _mm(jnp.tanh(wd), wup_ref[...]))) - 0.5
        lw = -jnp.exp(w_d)
        a_d = _sigmoid(vec(V_A0) + _mm(ad, aup_ref[...]))
        kk = k_d * vec(V_KK)
        k_d = k_d * (1.0 + (a_d - 1.0) * vec(V_KA))
        yield

        sq = head_sum(jnp.concatenate([q_b * q_b, k_b * k_b, kk * kk, q_a * k_a], axis=0))
        q_b = q_b * lax.rsqrt(sq[0:CHUNK] + L2_EPS) * (HEAD_DIM ** -0.5)
        k_b = k_b * lax.rsqrt(sq[CHUNK:2 * CHUNK] + L2_EPS)
        kk = kk * lax.rsqrt(sq[2 * CHUNK:3 * CHUNK] + L2_EPS)
        yield

        cums = _mm_const_l(ltri_ref[...], jnp.concatenate(
            [jnp.log(f_a), lw, g_gdn, g_ssd, g_gdn * strict, g_ssd * strict], axis=1))
        cum_a = cums[:, 0:GROUP_W]
        cum_d = cums[:, GROUP_W:2 * GROUP_W]
        cum_b = cums[:, 2 * GROUP_W:3 * GROUP_W]
        cum_c = cums[:, 3 * GROUP_W:4 * GROUP_W]
        gam_b = causal * jnp.exp(jnp.minimum(cums[:, 4 * GROUP_W:5 * GROUP_W], 0.0))
        gam_c = causal * jnp.exp(jnp.minimum(cums[:, 5 * GROUP_W:6 * GROUP_W], 0.0))
        yield

        kb = k_b * beta_e
        kq = _mm_nt(jnp.concatenate([kb, q_b], axis=0), bd(k_b))
        n_b = strict * kq[0:CHUNK] * gam_b
        qk_b = kq[CHUNK:2 * CHUNK] * gam_b
        yield

        p_inc = jnp.exp(cum_d)
        p_inv = jnp.exp(-cum_d)
        p_last = p_inc[CHUNK - 1:CHUNK, :]
        a_t = -kk * jnp.exp(cum_d - lw)
        b_t = kk * a_d * p_inv
        k_t = k_d * p_inv
        r_t = r_d * p_inc
        ar = jnp.concatenate([a_t, r_t], axis=0)
        arb = _mm_nt(ar, bd(b_t))
        ark = _mm_nt(ar, bd(k_t))
        ab = strict * arb[0:CHUNK]
        rb = causal * arb[CHUNK:2 * CHUNK]
        ak = strict * ark[0:CHUNK]
        rk = causal * ark[CHUNK:2 * CHUNK]
        a.update(rows=pl.ds(r0, CHUNK), systems=[n_b, -ab], beta_e=beta_e, q_b=q_b, k_b=k_b, v_b=v_b, kb=kb,
                 cum_b=cum_b, qk_b=qk_b, ar=ar, b_t=b_t, k_t=k_t, p_last=p_last, ak=ak, rb=rb, rk=rk, v_d=v_d,
                 r_d=r_d, k_d=k_d, gate_b=gates[:, GROUP_W:2 * GROUP_W], gate_d=gates[:, 3 * GROUP_W:4 * GROUP_W])
        yield SPLIT

        cum_last = cum_a[CHUNK - 1:CHUNK, :]
        q_in = q_a * jnp.exp(cum_a)
        k_out = k_a * jnp.exp(cum_last - cum_a)
        a_w = mask(M_EYE) * sq[3 * CHUNK:4 * CHUNK] + mask(M_LEVEL0) * _mm_nt(q_a * f_a, bd(k_a))
        yield
        for lev in range(1, N_LEVELS):
            b = 2 ** lev
            groups = []
            for g in range(CHUNK // SUBLANES):
                first = g * SUBLANES
                if 2 * b >= SUBLANES:
                    r = first // (2 * b) * (2 * b) + b - 1
                    groups.append(jnp.broadcast_to(cum_a[r:r + 1, :], (SUBLANES, GROUP_W)))
                else:
                    groups.append(jnp.where(
                        sub_r < 2 * b,
                        jnp.broadcast_to(cum_a[first + b - 1:first + b, :], (SUBLANES, GROUP_W)),
                        jnp.broadcast_to(cum_a[first + 3 * b - 1:first + 3 * b, :], (SUBLANES, GROUP_W))))
            e = jnp.exp(-jnp.abs(cum_a - jnp.concatenate(groups, axis=0)))
            a_w = a_w + mask(M_LEVEL0 + lev) * _mm_nt(q_a * e, bd(k_a * e))
            yield
        st_a = s_hg[...]
        o_a = _mm(a_w, bd(v_a)) + _mm_nt(q_in, st_a)
        s_hg[...] = st_a * jnp.exp(cum_last) + bdmf_ref[...] * _mm_tn(v_a, k_out)
        yield

        cb = _mm_nt(c_c, bd(b_c, GROUP_W // SSM_GROUPS))
        xv = x_c * dt_e
        c_last = cum_c[CHUNK - 1:CHUNK, :]
        st_c = s_ssm[...]
        o_c = _mm(cb * gam_c, bd(xv)) + jnp.exp(cum_c) * _mm(c_c, st_c)
        s_ssm[...] = st_c * jnp.exp(c_last) + grpmf_ref[...] * _mm_tn(b_c, xv * jnp.exp(c_last - cum_c))
        yield

        rows = a["rows"]
        y_a = o_a * lax.rsqrt(head_sum(o_a * o_a) * (1.0 / HEAD_DIM) + NORM_EPS) * vec(V_HG_G)
        y_ref[rows, 0:GROUP_W] = (y_a * _silu(gates[:, 0:GROUP_W])).astype(y_ref.dtype)
        yv = (o_c + x_c * vec(V_SSM_D)) * _silu(gates[:, 2 * GROUP_W:3 * GROUP_W])
        gw = GROUP_W // SSM_GROUPS
        for grp in range(SSM_GROUPS):
            yg = yv[:, grp * gw:(grp + 1) * gw]
            yg = yg * lax.rsqrt(jnp.mean(yg * yg, axis=-1, keepdims=True) + NORM_EPS)
            yg = yg * vec_ref[V_SSM_G:V_SSM_G + 1, grp * gw:(grp + 1) * gw]
            y_ref[rows, 2 * GROUP_W + grp * gw:2 * GROUP_W + (grp + 1) * gw] = yg.astype(y_ref.dtype)

    def phase_b(a):
        x_b, x_d = a["inverses"]
        (rows, beta_e, q_b, k_b, v_b, kb, cum_b, qk_b, ar, b_t, k_t, p_last, ak, rb, rk, v_d, r_d, k_d) = (
            a[n] for n in ("rows", "beta_e", "q_b", "k_b", "v_b", "kb", "cum_b", "qk_b", "ar", "b_t", "k_t",
                           "p_last", "ak", "rb", "rk", "v_d", "r_d", "k_d"))
        ecum_b = jnp.exp(cum_b)
        b_last = cum_b[CHUNK - 1:CHUNK, :]
        st_b = s_gdn[...]
        st_d = s_rwkv[...]
        uu = mm_bd(x_b, v_b * beta_e)
        ww = mm_bd(x_b, kb * ecum_b)
        rhs_d = _mm_nt(ar, st_d)
        yield
        wq = _mm(jnp.concatenate([ww, q_b * ecum_b], axis=0), st_b)
        v_new = uu - wq[0:CHUNK]
        bd_vd = bd(v_d)
        rhs_u = rhs_d[0:CHUNK] + _mm(ak, bd_vd)
        yield
        u_d = mm_bd(x_d, rhs_u)
        yield
        o_b = wq[CHUNK:2 * CHUNK] + _mm(qk_b, bd(v_new))
        o_d = rhs_d[CHUNK:2 * CHUNK] + _mm(rb, bd(u_d)) + _mm(rk, bd_vd)
        yield
        bdm = bdmf_ref[...]
        s_gdn[...] = st_b * jnp.exp(b_last) + bdm * _mm_tn(k_b * jnp.exp(b_last - cum_b), v_new)
        s_rwkv[...] = st_d * p_last + bdm * _mm_tn(jnp.concatenate([u_d, v_d], axis=0),
                                                   jnp.concatenate([b_t * p_last, k_t * p_last], axis=0))
        yield

        sums = head_sum(jnp.concatenate([o_b * o_b, o_d, r_d * k_d * vec(V_RK)], axis=0))
        y_b = o_b * lax.rsqrt(sums[0:CHUNK] * (1.0 / HEAD_DIM) + NORM_EPS) * vec(V_GDN_G)
        oc = o_d - sums[CHUNK:2 * CHUNK] * (1.0 / HEAD_DIM)
        var = head_sum(oc * oc) * (1.0 / HEAD_DIM)
        y_d = oc * lax.rsqrt(var + RWKV_GN_EPS) * vec(V_LNW) + vec(V_LNB) + sums[2 * CHUNK:3 * CHUNK] * v_d
        y_ref[rows, GROUP_W:2 * GROUP_W] = (y_b * _silu(a["gate_b"])).astype(y_ref.dtype)
        y_ref[rows, 3 * GROUP_W:4 * GROUP_W] = (y_d * _silu(a["gate_d"])).astype(y_ref.dtype)

    parts = [{} for _ in range(n_chunks)]
    a_gens = {}

    def phase_a_first(k):
        a_gens[k] = phase_a(k, parts[k])
        for marker in a_gens[k]:
            if marker is SPLIT:
                return
            yield

    threads = (
        (0, PIPE_PERIODS[0], phase_a_first),
        (1, PIPE_PERIODS[1], lambda k: a_gens[k]),
        (1, PIPE_PERIODS[2], lambda k: tri_inverse_levels(parts[k], 0, INVERSE_SPLIT)),
        (2, PIPE_PERIODS[3], lambda k: tri_inverse_levels(parts[k], INVERSE_SPLIT, N_LEVELS)),
        (3, PIPE_PERIODS[4], lambda k: phase_b(parts[k])),
    )
    max_lag = max(lag for lag, _, _ in threads)
    n_slots = n_chunks + max_lag
    background = []
    for j in range(1, n_blocks):
        background.append(((j - 1) * chunks_per_block, j * chunks_per_block - 1, RIDER_PERIODS[0], in_projection(j)))
    for j in range(n_blocks):
        first = (j + 1) * chunks_per_block + max_lag
        background.append((min(first, n_slots), n_slots - 1, RIDER_PERIODS[1], out_projection(j)))
    background.append((chunks_per_block, n_slots - 1, RIDER_PERIODS[2], in_projection(0, next_tile=True)))
    for slot in range(n_slots):
        live = [(period, make(slot - lag)) for lag, period, make in threads if 0 <= slot - lag < n_chunks]
        riders = [task for task in background if task[0] <= slot <= task[1]]
        tick = 0
        while live:
            for entry in list(live):
                if tick % entry[0] == 0 and next(entry[1], "done") == "done":
                    live.remove(entry)
            for _, _, period, gen in riders:
                if tick % period == 0:
                    next(gen, None)
            tick += 1
        for _, last, _, gen in riders:
            if last == slot:
                for _ in gen:
                    pass
    for first, _, _, gen in background:
        if first >= n_slots:
            for _ in gen:
                pass


def _layer_call(layer, h, *params):
    b, l, d = h.shape
    tl = min(TIME_TILE, l)
    assert l % tl == 0 and tl % CHUNK == 0 and d == D_MODEL
    n_chunks = tl // CHUNK
    bd_state = pltpu.VMEM((N_HEADS * HEAD_DIM, N_HEADS * HEAD_DIM), F32)

    def full(a):
        return pl.BlockSpec(a.shape, lambda i, j: (0,) * a.ndim)

    rows_next = min(PROJ_ROWS, tl)
    n_tiles = l // tl

    def next_block(i, j):
        wraps = j + 1 == n_tiles
        return (jnp.where(wraps, jnp.minimum(i + 1, b - 1), i), jnp.where(wraps, 0, (j + 1) * (tl // rows_next)), 0)

    return pl.pallas_call(
        functools.partial(_layer_kernel, layer, n_chunks),
        grid=(b, l // tl),
        in_specs=[pl.BlockSpec((None, tl, d), lambda i, j: (i, j, 0)),
                  pl.BlockSpec((None, rows_next, d), next_block)] + [full(a) for a in params],
        out_specs=pl.BlockSpec((None, tl, d), lambda i, j: (i, j, 0)),
        out_shape=jax.ShapeDtypeStruct(h.shape, h.dtype),
        scratch_shapes=[
            pltpu.VMEM((tl + TAIL, PROJ_COLS), F32),
            pltpu.VMEM((tl, D_MODEL), BF16),
            bd_state,
            bd_state,
            bd_state,
            bd_state,
        ],
        compiler_params=pltpu.CompilerParams(
            dimension_semantics=("arbitrary", "arbitrary"),
            vmem_limit_bytes=V7X_VMEM_LIMIT_BYTES),
        name=f"hybrid_layer{layer}",
    )(h, h, *params)


W_IN_PIECES = (
    ((0, 1536),),
    ((1544, 2312),),
    ((2316, 3212),),
    ((3212, 4236),),
    ((1536, 1544), (2312, 2316), (2312, 2316)),
)


def _repack_kernel(w_ref, *out_refs):
    w = w_ref[...]
    for pieces, o_ref in zip(W_IN_PIECES, out_refs):
        cols = [w[:, a:b] for a, b in pieces]
        used = sum(b - a for a, b in pieces)
        if used < o_ref.shape[1]:
            cols.append(jnp.zeros((w.shape[0], o_ref.shape[1] - used), w.dtype))
        o_ref[...] = (cols[0] if len(cols) == 1 else jnp.concatenate(cols, axis=1)).astype(o_ref.dtype)


def _split_w_in(w_in, layer):
    _, d, cols = w_in.shape
    widths = [-(-sum(b - a for a, b in p) // 128) * 128 for p in W_IN_PIECES]
    assert sum(widths) == PROJ_COLS and d % CAST_ROWS == 0
    return pl.pallas_call(
        _repack_kernel, grid=(d // CAST_ROWS,),
        in_specs=[pl.BlockSpec((None, CAST_ROWS, cols), lambda i: (layer, i, 0))],
        out_specs=[pl.BlockSpec((CAST_ROWS, w), lambda i: (i, 0)) for w in widths],
        out_shape=[jax.ShapeDtypeStruct((d, w), BF16) for w in widths], name="repack_w_in",
    )(w_in)


def _constants():
    ri = lax.broadcasted_iota(jnp.int32, (CHUNK, CHUNK), 0)
    ci = lax.broadcasted_iota(jnp.int32, (CHUNK, CHUNK), 1)
    ltri = (ri >= ci).astype(BF16)
    r2 = lax.broadcasted_iota(jnp.int32, (GROUP_W, GROUP_W), 0)
    c2 = lax.broadcasted_iota(jnp.int32, (GROUP_W, GROUP_W), 1)
    bdm = ((r2 // HEAD_DIM) == (c2 // HEAD_DIM)).astype(BF16)
    gw = GROUP_W // SSM_GROUPS
    grpm = ((r2 // gw) == (c2 // gw)).astype(F32)
    er = lax.broadcasted_iota(jnp.int32, (SMALL_W, 4 * GROUP_W), 0)
    ec = lax.broadcasted_iota(jnp.int32, (SMALL_W, 4 * GROUP_W), 1)
    eexp = (er == ec // HEAD_DIM).astype(BF16)
    levels = []
    for lev in range(N_LEVELS):
        s = 2 ** lev
        levels.append(((ri // (2 * s)) == (ci // (2 * s))) & (((ri // s) % 2) == 1) & (((ci // s) % 2) == 0))
    masks = jnp.stack([jnp.tile(m.astype(F32), (1, N_HEADS)) for m in [ri >= ci, ri > ci, ri == ci] + levels])
    return ltri, bdm, bdm.astype(F32), grpm, eexp, masks


def kernel(x, pre_norm_g, w_in, hgrn_lower_bounds, hgrn_norm_g, gdn_conv_w, gdn_a_log, gdn_dt_bias, gdn_norm_g,
           ssm_conv_w, ssm_conv_b, ssm_a_log, ssm_dt_bias, ssm_d, ssm_norm_g, rwkv_mu, rwkv_w0, rwkv_w_up,
           rwkv_a0, rwkv_a_up, rwkv_k_k, rwkv_k_a, rwkv_r_k, rwkv_ln_w, rwkv_ln_b, w_out, post_norm_g):
    depth = w_in.shape[0]
    consts = _constants()
    zeros4 = jnp.zeros((N_HEADS,), F32)
    ones4 = jnp.ones((N_HEADS,), F32)
    pad = jnp.zeros((SMALL_W - 4 * N_HEADS,), F32)
    h = x
    for l in range(depth):
        vecs = jnp.stack([
            jnp.tile(hgrn_norm_g[l], N_HEADS), jnp.tile(gdn_norm_g[l], N_HEADS),
            jnp.repeat(ssm_d[l], HEAD_DIM), ssm_norm_g[l], rwkv_w0[l], rwkv_a0[l], rwkv_k_k[l],
            rwkv_k_a[l], rwkv_r_k[l], rwkv_ln_w[l], rwkv_ln_b[l]] + [hgrn_lower_bounds[i] for i in range(depth)])
        small = jnp.stack([
            jnp.concatenate([zeros4, gdn_dt_bias[l], ssm_dt_bias[l], ssm_dt_bias[l], pad]),
            jnp.concatenate([zeros4, gdn_a_log[l], zeros4, ssm_a_log[l], pad]),
            jnp.concatenate([zeros4, zeros4, ones4, zeros4, pad])])
        h = _layer_call(
            l, h, pre_norm_g[l][None, :], *_split_w_in(w_in, l), w_out[l].astype(BF16), post_norm_g[l][None, :],
            vecs.astype(F32), small.astype(F32), gdn_conv_w[l], ssm_conv_w[l], ssm_conv_b[l][None, :],
            rwkv_mu[l][None, :], rwkv_w_up[l].astype(BF16), rwkv_a_up[l].astype(BF16), *consts)
    return h
```
